```python
import jax, jax.numpy as jnp
from jax import lax
import numpy as np

D_MODEL = 4096
BATCH = 16
SEQ = 256
DEPTH = 2
DEC_BATCH = 4
DEC_SEQ = 4096
PAST_LEN = 256

GRID_W = 64
Q_BLOCK = 128
CHUNK = 128
ROPE_THETA = 10000.0
NORM_EPS = 1e-6
DEEPNORM_ALPHA = (2 * DEPTH) ** 0.25
DEEPNORM_BETA = (8 * DEPTH) ** -0.25

A_WIDTH = D_MODEL // 2
A_GROUPS = 16
A_GROUP_DIM = A_WIDTH // A_GROUPS
HEAD_DIM = 128
GQA_HEADS = D_MODEL // 256
GQA_KV_HEADS = GQA_HEADS // 4
GQA_GROUP = GQA_HEADS // GQA_KV_HEADS
GQA_Q_WIDTH = GQA_HEADS * HEAD_DIM
GQA_KV_WIDTH = GQA_KV_HEADS * HEAD_DIM
MLA_HEADS = D_MODEL // 256
MLA_Q_RANK = D_MODEL // 4
MLA_KV_RANK = 512
MLA_NOPE_DIM = 128
MLA_ROPE_DIM = 64
MLA_V_DIM = 128
MLA_SCALE = (MLA_NOPE_DIM + MLA_ROPE_DIM) ** -0.5
FFN_HIDDEN = ((8 * D_MODEL + 3 * 256 - 1) // (3 * 256)) * 256
IN_WIDTHS = (A_WIDTH, A_WIDTH, GQA_Q_WIDTH, GQA_KV_WIDTH, GQA_KV_WIDTH, MLA_Q_RANK, MLA_KV_RANK, MLA_ROPE_DIM, 3 * D_MODEL)
IN_COLS = sum(IN_WIDTHS)
IN_SPLITS = tuple(int(s) for s in np.cumsum(IN_WIDTHS)[:-1])

kernel_name = 'hybrid_diffusion_gmlp_gqa_mla_step'


def layer_norm(x, g, b):
    xf = x.astype(jnp.float32)
    mu = jnp.mean(xf, axis=-1, keepdims=True)
    xc = xf - mu
    var = jnp.mean(xc * xc, axis=-1, keepdims=True)
    return (xc * lax.rsqrt(var + NORM_EPS) * g.astype(jnp.float32) + b.astype(jnp.float32)).astype(x.dtype)


def rms_norm(x, g):
    xf = x.astype(jnp.float32)
    y = xf * lax.rsqrt(jnp.mean(xf * xf, axis=-1, keepdims=True) + NORM_EPS)
    return (y * g.astype(jnp.float32)).astype(x.dtype)


def rope_tables(n, dim):
    rows = n // GRID_W
    row = jnp.repeat(jnp.arange(rows, dtype=jnp.float32), GRID_W)
    col = jnp.tile(jnp.arange(GRID_W, dtype=jnp.float32), rows)
    quarter = dim // 4
    freqs = ROPE_THETA ** (-jnp.arange(quarter, dtype=jnp.float32) / quarter)
    ra = row[:, None] * freqs[None, :]
    ca = col[:, None] * freqs[None, :]
    ang = jnp.concatenate([ra, ra, ca, ca], axis=-1)
    return jnp.cos(ang), jnp.sin(ang)


def apply_rope(x, cos, sin):
    shape = (1, x.shape[1]) + (1,) * (x.ndim - 3) + (x.shape[-1],)
    cos = cos.reshape(shape).astype(x.dtype)
    sin = sin.reshape(shape).astype(x.dtype)
    xs = x.reshape(*x.shape[:-1], 2, 2, x.shape[-1] // 4)
    rot = jnp.stack([-xs[..., 1, :], xs[..., 0, :]], axis=-2).reshape(x.shape)
    return x * cos + rot * sin


def sweep_query_blocks(block_fn, queries, keys):
    b, n = queries[0].shape[:2]
    nb = n // Q_BLOCK
    q_blocks = tuple(jnp.moveaxis(q.reshape(b, nb, Q_BLOCK, *q.shape[2:]), 1, 0) for q in queries)
    out = lax.map(lambda qb: block_fn(*qb, *keys), q_blocks)
    return jnp.moveaxis(out, 0, 1).reshape(b, n, *out.shape[3:])


def gqa_block(q, k, v):
    s = jnp.einsum('bqngd,bsnd->bngqs', q, k).astype(jnp.float32) * (HEAD_DIM ** -0.5)
    p = jax.nn.softmax(s, axis=-1).astype(v.dtype)
    return jnp.einsum('bngqs,bsnd->bqngd', p, v)


def mla_block(q_nope, q_pe, k_nope, k_pe, v):
    s = jnp.einsum('bqhd,bshd->bhqs', q_nope, k_nope) + jnp.einsum('bqhr,bsr->bhqs', q_pe, k_pe)
    p = jax.nn.softmax(s.astype(jnp.float32) * MLA_SCALE, axis=-1).astype(v.dtype)
    return jnp.einsum('bhqs,bshd->bqhd', p, v)


def chunk_gating(u, v, lw):
    b, n, _ = v.shape
    v = layer_norm(v, lw['sgu_ln_g'], lw['sgu_ln_b'])
    vg = v.reshape(b, n // CHUNK, CHUNK, A_GROUPS, A_GROUP_DIM)
    s = jnp.einsum('gpq,bcqgd->bcpgd', lw['w_s'], vg) + lw['b_s'].T[None, None, :, :, None]
    return u * s.reshape(b, n, A_WIDTH)


def mixers(h, rope, ctx, lw):
    b, n, _ = h.shape
    u, va, qb, kb, vb, cq, ckv, kpe, gates = jnp.split(h @ lw['w_in'], IN_SPLITS, axis=-1)
    y_a = chunk_gating(u, va, lw)
    qb = rms_norm(qb.reshape(b, n, GQA_HEADS, HEAD_DIM), lw['q_norm_g'])
    kb = rms_norm(kb.reshape(b, n, GQA_KV_HEADS, HEAD_DIM), lw['k_norm_g'])
    vb = vb.reshape(b, n, GQA_KV_HEADS, HEAD_DIM)
    qc = (rms_norm(cq, lw['mla_q_norm_g']) @ lw['w_uq']).reshape(b, n, MLA_HEADS, MLA_NOPE_DIM + MLA_ROPE_DIM)
    q_nope, q_pe = qc[..., :MLA_NOPE_DIM], qc[..., MLA_NOPE_DIM:]
    ckv = rms_norm(ckv, lw['mla_kv_norm_g'])
    if rope is not None:
        cos_g, sin_g, cos_m, sin_m = rope
        qb = apply_rope(qb, cos_g, sin_g)
        kb = apply_rope(kb, cos_g, sin_g)
        q_pe = apply_rope(q_pe, cos_m, sin_m)
        kpe = apply_rope(kpe, cos_m, sin_m)
    own = (kb, vb, ckv, kpe)
    if ctx is not None:
        kb, vb, ckv, kpe = (jnp.concatenate([c_t, o_t], axis=1) for c_t, o_t in zip(ctx, own))
    y_b = sweep_query_blocks(gqa_block, (qb.reshape(b, n, GQA_KV_HEADS, GQA_GROUP, HEAD_DIM),), (kb, vb))
    y_b = y_b.reshape(b, n, GQA_Q_WIDTH)
    kv = (ckv @ lw['w_ukv']).reshape(b, -1, MLA_HEADS, MLA_NOPE_DIM + MLA_V_DIM)
    y_c = sweep_query_blocks(mla_block, (q_nope, q_pe), (kv[..., :MLA_NOPE_DIM], kpe, kv[..., MLA_NOPE_DIM:]))
    y_c = y_c.reshape(b, n, MLA_HEADS * MLA_V_DIM)
    g_a, g_b, g_c = jnp.split(jax.nn.sigmoid(gates), 3, axis=-1)
    merged = g_a * (y_a @ lw['w_pa']) + g_b * (y_b @ lw['w_pb']) + g_c * (y_c @ lw['w_pc'])
    return merged @ lw['w_o'], own


def swiglu(h, lw):
    return (jax.nn.silu(h @ lw['w_gate']) * (h @ lw['w_up'])) @ lw['w_down']


def trunk_layer(x, cond, rope, ctx, lw):
    mods = (jax.nn.silu(cond) @ lw['w_ada'] + lw['b_ada'])[:, None, :]
    sh1, sc1, g1, sh2, sc2, g2 = jnp.split(mods, 6, axis=-1)
    mix, own = mixers(x * (1 + sc1) + sh1, rope, ctx, lw)
    x = layer_norm(DEEPNORM_ALPHA * x + g1 * mix, lw['ln1_g'], lw['ln1_b'])
    ff = swiglu(x * (1 + sc2) + sh2, lw)
    x = layer_norm(DEEPNORM_ALPHA * x + g2 * ff, lw['ln2_g'], lw['ln2_b'])
    return x, own


def setup_inputs(seed: int = 0) -> dict:
    key = jax.random.key(seed)
    ks = iter(jax.random.split(key, 40))
    f32 = jnp.float32

    def nrm(shape, scale):
        return jax.random.normal(next(ks), shape, f32) * scale

    def gain(shape):
        return 1.0 + nrm(shape, 0.02)

    L = DEPTH
    return {
        'x_prompt': nrm((BATCH, SEQ, D_MODEL), 1.0),
        'x_sample': nrm((DEC_BATCH, DEC_SEQ, D_MODEL), 1.0),
        'cache_k': nrm((DEC_BATCH, L, PAST_LEN, GQA_KV_HEADS, HEAD_DIM), 1.0),
        'cache_v': nrm((DEC_BATCH, L, PAST_LEN, GQA_KV_HEADS, HEAD_DIM), 1.0),
        'cache_ckv': nrm((DEC_BATCH, L, PAST_LEN, MLA_KV_RANK), 1.0),
        'cache_kpe': nrm((DEC_BATCH, L, PAST_LEN, MLA_ROPE_DIM), 1.0),
        'c': nrm((DEC_BATCH, D_MODEL), 1.0),
        'c_ctx': nrm((D_MODEL,), 1.0),
        'w_ada': nrm((L, D_MODEL, 6 * D_MODEL), D_MODEL ** -0.5),
        'b_ada': nrm((L, 6 * D_MODEL), 0.02),
        'w_in': nrm((L, D_MODEL, IN_COLS), D_MODEL ** -0.5),
        'sgu_ln_g': gain((L, A_WIDTH)),
        'sgu_ln_b': nrm((L, A_WIDTH), 0.02),
        'w_s': nrm((L, A_GROUPS, CHUNK, CHUNK), CHUNK ** -0.5),
        'b_s': gain((L, A_GROUPS, CHUNK)),
        'q_norm_g': gain((L, HEAD_DIM)),
        'k_norm_g': gain((L, HEAD_DIM)),
        'mla_q_norm_g': gain((L, MLA_Q_RANK)),
        'mla_kv_norm_g': gain((L, MLA_KV_RANK)),
        'w_uq': nrm((L, MLA_Q_RANK, MLA_HEADS * (MLA_NOPE_DIM + MLA_ROPE_DIM)), MLA_Q_RANK ** -0.5),
        'w_ukv': nrm((L, MLA_KV_RANK, MLA_HEADS * (MLA_NOPE_DIM + MLA_V_DIM)), MLA_KV_RANK ** -0.5),
        'w_pa': nrm((L, A_WIDTH, D_MODEL), A_WIDTH ** -0.5),
        'w_pb': nrm((L, GQA_Q_WIDTH, D_MODEL), GQA_Q_WIDTH ** -0.5),
        'w_pc': nrm((L, MLA_HEADS * MLA_V_DIM, D_MODEL), (MLA_HEADS * MLA_V_DIM) ** -0.5),
        'w_o': nrm((L, D_MODEL, D_MODEL), DEEPNORM_BETA * D_MODEL ** -0.5),
        'ln1_g': gain((L, D_MODEL)),
        'ln1_b': nrm((L, D_MODEL), 0.02),
        'ln2_g': gain((L, D_MODEL)),
        'ln2_b': nrm((L, D_MODEL), 0.02),
        'w_gate': nrm((L, D_MODEL, FFN_HIDDEN), D_MODEL ** -0.5),
        'w_up': nrm((L, D_MODEL, FFN_HIDDEN), D_MODEL ** -0.5),
        'w_down': nrm((L, FFN_HIDDEN, D_MODEL), DEEPNORM_BETA * FFN_HIDDEN ** -0.5),
    }


def reference(x_prompt, x_sample, cache_k, cache_v, cache_ckv, cache_kpe, c, c_ctx,
              w_ada, b_ada, w_in, sgu_ln_g, sgu_ln_b, w_s, b_s, q_norm_g, k_norm_g,
              mla_q_norm_g, mla_kv_norm_g, w_uq, w_ukv, w_pa, w_pb, w_pc, w_o,
              ln1_g, ln1_b, ln2_g, ln2_b, w_gate, w_up, w_down):
    n_lat = x_sample.shape[1]
    cos_g, sin_g = rope_tables(n_lat, HEAD_DIM)
    cos_m, sin_m = rope_tables(n_lat, MLA_ROPE_DIM)
    rope = (cos_g, sin_g, cos_m, sin_m)
    cond_ctx = c_ctx[None, :]
    xp, xs = x_prompt, x_sample
    new_k, new_v, new_ckv, new_kpe = [], [], [], []
    for layer in range(DEPTH):
        lw = {
            'w_ada': w_ada[layer], 'b_ada': b_ada[layer], 'w_in': w_in[layer],
            'sgu_ln_g': sgu_ln_g[layer], 'sgu_ln_b': sgu_ln_b[layer], 'w_s': w_s[layer], 'b_s': b_s[layer],
            'q_norm_g': q_norm_g[layer], 'k_norm_g': k_norm_g[layer],
            'mla_q_norm_g': mla_q_norm_g[layer], 'mla_kv_norm_g': mla_kv_norm_g[layer],
            'w_uq': w_uq[layer], 'w_ukv': w_ukv[layer],
            'w_pa': w_pa[layer], 'w_pb': w_pb[layer], 'w_pc': w_pc[layer], 'w_o': w_o[layer],
            'ln1_g': ln1_g[layer], 'ln1_b': ln1_b[layer], 'ln2_g': ln2_g[layer], 'ln2_b': ln2_b[layer],
            'w_gate': w_gate[layer], 'w_up': w_up[layer], 'w_down': w_down[layer],
        }
        xp, (k_l, v_l, ckv_l, kpe_l) = trunk_layer(xp, cond_ctx, None, None, lw)
        new_k.append(k_l)
        new_v.append(v_l)
        new_ckv.append(ckv_l)
        new_kpe.append(kpe_l)
        cached = (cache_k[:, layer], cache_v[:, layer], cache_ckv[:, layer], cache_kpe[:, layer])
        xs, _ = trunk_layer(xs, c, rope, cached, lw)
    return (xp, xs, jnp.stack(new_k, axis=1), jnp.stack(new_v, axis=1), jnp.stack(new_ckv, axis=1), jnp.stack(new_kpe, axis=1))
```

```python
import functools

import jax
import jax.numpy as jnp
from jax import lax
from jax.experimental import pallas as pl
from jax.experimental.pallas import tpu as pltpu

F32 = jnp.float32
BF16 = jnp.bfloat16

NORM_EPS = 1e-6
ROPE_THETA = 10000.0
GRID_W = 64

LANES = 128
MXU_DIM = 256
VMEM_LIMIT_BYTES = 56 * 1024 * 1024


def _cparams(n_axes):
    return pltpu.CompilerParams(
        dimension_semantics=("arbitrary",) * n_axes,
        vmem_limit_bytes=VMEM_LIMIT_BYTES,
    )


def _blk(n, pref):
    b = min(n, pref)
    while n % b:
        b //= 2
    return b


def _rms(a, g):
    ms = jnp.mean(a * a, axis=-1, keepdims=True)
    return a * lax.rsqrt(ms + NORM_EPS) * g


def _rope(y, cos, sin_signed, quarter):
    n = y.shape[-1]
    lane = lax.broadcasted_iota(jnp.int32, y.shape, 1)
    first = (lane & quarter) == 0
    rot = jnp.where(first, pltpu.roll(y, n - quarter, axis=1), pltpu.roll(y, quarter, axis=1))
    return y * cos + rot * sin_signed


def _sigmoid(x):
    return 1.0 / (1.0 + jnp.exp(-x))


def _layer_norm(z, g, b):
    mu = jnp.mean(z, axis=-1, keepdims=True)
    zc = z - mu
    var = jnp.mean(zc * zc, axis=-1, keepdims=True)
    return zc * lax.rsqrt(var + NORM_EPS) * g + b


def _ada_kernel(c_ref, w_ref, b_ref, o_ref):
    c = c_ref[...]
    s = (c * _sigmoid(c)).astype(BF16)
    o_ref[...] = jnp.dot(s, w_ref[...].astype(BF16), preferred_element_type=F32) + b_ref[...]


def _ada(cond, w_ada, b_ada):
    L, D, N = w_ada.shape
    R = cond.shape[0]
    bn = _blk(N, 512)
    return pl.pallas_call(
        _ada_kernel,
        grid=(L, N // bn),
        in_specs=[
            pl.BlockSpec((R, D), lambda l, j: (0, 0)),
            pl.BlockSpec((None, D, bn), lambda l, j: (l, 0, j)),
            pl.BlockSpec((None, 1, bn), lambda l, j: (l, 0, j)),
        ],
        out_specs=pl.BlockSpec((None, R, bn), lambda l, j: (l, 0, j)),
        out_shape=jax.ShapeDtypeStruct((L, R, N), F32),
        compiler_params=_cparams(2),
        name="ada_mod",
    )(cond, w_ada, b_ada.reshape(L, 1, N))


def _modulate_kernel(x_ref, sc_ref, sh_ref, h_ref):
    h_ref[...] = (x_ref[...] * (1.0 + sc_ref[...]) + sh_ref[...]).astype(h_ref.dtype)


def _mod_spec(D, k, rows_per_group, bt):
    return pl.BlockSpec((None, 1, D), lambda i, *_: ((i * bt) // rows_per_group, 0, k))


def _modulate(x, mods, k_sc, k_sh, rows_per_group):
    M, D = x.shape
    bt = _blk(rows_per_group, 512)
    return pl.pallas_call(
        _modulate_kernel,
        grid=(M // bt,),
        in_specs=[
            pl.BlockSpec((bt, D), lambda i: (i, 0)),
            _mod_spec(D, k_sc, rows_per_group, bt),
            _mod_spec(D, k_sh, rows_per_group, bt),
        ],
        out_specs=pl.BlockSpec((bt, D), lambda i: (i, 0)),
        out_shape=jax.ShapeDtypeStruct((M, D), BF16),
        compiler_params=_cparams(1),
        name="modulate",
    )(x, mods, mods)


def _ln_kernel(x_ref, y_ref, gate_ref, g_ref, b_ref, *rest, alpha, with_h):
    z = alpha * x_ref[...] + gate_ref[...] * y_ref[...]
    xn = _layer_norm(z, g_ref[...], b_ref[...])
    if with_h:
        sc_ref, sh_ref, xo_ref, h_ref = rest
        xo_ref[...] = xn
        h_ref[...] = (xn * (1.0 + sc_ref[...]) + sh_ref[...]).astype(h_ref.dtype)
    else:
        (xo_ref,) = rest
        xo_ref[...] = xn


def _ln_residual(x, y, mods, k_gate, ln_g, ln_b, alpha, rows_per_group, nxt=None):
    M, D = x.shape
    bt = _blk(rows_per_group, 256)
    row = pl.BlockSpec((bt, D), lambda i: (i, 0))
    vec = pl.BlockSpec((1, D), lambda i: (0, 0))
    in_specs = [row, row, _mod_spec(D, k_gate, rows_per_group, bt), vec, vec]
    args = [x, y, mods, ln_g.reshape(1, D), ln_b.reshape(1, D)]
    out_shape = [jax.ShapeDtypeStruct((M, D), F32)]
    out_specs = [row]
    if nxt is not None:
        mods_n, k_sc, k_sh = nxt
        in_specs += [_mod_spec(D, k_sc, rows_per_group, bt), _mod_spec(D, k_sh, rows_per_group, bt)]
        args += [mods_n, mods_n]
        out_shape.append(jax.ShapeDtypeStruct((M, D), BF16))
        out_specs.append(row)
    res = pl.pallas_call(
        functools.partial(_ln_kernel, alpha=alpha, with_h=nxt is not None),
        grid=(M // bt,),
        in_specs=in_specs,
        out_specs=out_specs,
        out_shape=out_shape,
        compiler_params=_cparams(1),
        name="ln_residual",
    )(*args)
    return res if nxt is not None else (res[0], None)


def _mm_call(kernel, x, ws, n_cols, bm, bn, extras, extra_specs, out_dtypes, name, out_cols=None):
    M, K = x.shape
    out_cols = n_cols if out_cols is None else out_cols
    in_specs = [pl.BlockSpec((bm, K), lambda i, j: (i, 0))]
    in_specs += [pl.BlockSpec((w.shape[0], bn), lambda i, j: (0, j)) for w in ws]
    in_specs += list(extra_specs)
    out_shape = [jax.ShapeDtypeStruct((M, out_cols), dt) for dt in out_dtypes]
    out_specs = [pl.BlockSpec((bm, bn), lambda i, j: (i, j)) for _ in out_dtypes]
    return pl.pallas_call(
        kernel,
        grid=(M // bm, n_cols // bn),
        in_specs=in_specs,
        out_specs=out_specs,
        out_shape=out_shape,
        compiler_params=_cparams(2),
        name=name,
    )(x, *ws, *extras)


def _mm_plain_kernel(x_ref, w_ref, o_ref):
    o_ref[...] = jnp.dot(x_ref[...], w_ref[...], preferred_element_type=F32).astype(o_ref.dtype)


def _mm_plain(x, w, out_dtype, name, bm=1024, bn=1024):
    M, K = x.shape
    N = w.shape[1]
    bm, bn = _blk(M, bm), _blk(N, bn)
    return _mm_call(_mm_plain_kernel, x, [w], N, bm, bn, [], [], [out_dtype], name)[0]


def _rope_specs(rope, bm, rows_per_batch):
    if rope is None:
        return [], []
    nb = rows_per_batch // bm
    spec = pl.BlockSpec((bm, LANES), lambda i, j: (i % nb, 0))
    return [rope[0], rope[1]], [spec, spec]


def _mm_heads_kernel(x_ref, w_ref, g_ref, *rest, n_norm, rope, quarter):
    if rope:
        cos_ref, sin_ref, *outs = rest
    else:
        outs = rest
    acc = jnp.dot(x_ref[...], w_ref[...], preferred_element_type=F32)
    for h in range(acc.shape[1] // LANES):
        a = acc[:, h * LANES:(h + 1) * LANES]
        if h < n_norm:
            a = _rms(a, g_ref[...])
            if rope:
                a = _rope(a, cos_ref[...], sin_ref[...], quarter)
        for o in outs:
            o[:, h * LANES:(h + 1) * LANES] = a.astype(o.dtype)


def _mm_heads(x, w, gain, rope, rows_per_batch, n_norm, out_dtypes, name, bm=1024, bn=1024):
    M, K = x.shape
    N = w.shape[1]
    bm, bn = _blk(M if rope is None else rows_per_batch, bm), _blk(N, bn)
    rargs, rspecs = _rope_specs(rope, bm, rows_per_batch)
    kern = functools.partial(_mm_heads_kernel, n_norm=n_norm, rope=rope is not None,
                             quarter=LANES // 4)
    return _mm_call(kern, x, [w], N, bm, bn,
                    [gain.reshape(1, LANES)] + rargs,
                    [pl.BlockSpec((1, LANES), lambda i, j: (0, 0))] + rspecs,
                    out_dtypes, name)


def _mm_rms_kernel(x_ref, w_ref, g_ref, o_ref):
    acc = jnp.dot(x_ref[...], w_ref[...], preferred_element_type=F32)
    o_ref[...] = _rms(acc, g_ref[...]).astype(o_ref.dtype)


def _mm_rms(x, w, gain, out_dtype, name, bm=1024):
    M, K = x.shape
    N = w.shape[1]
    bm = _blk(M, bm)
    return _mm_call(_mm_rms_kernel, x, [w], N, bm, N, [gain.reshape(1, N)],
                    [pl.BlockSpec((1, N), lambda i, j: (0, 0))], [out_dtype], name)[0]


def _mm_ckv_kernel(x_ref, w_ref, g_ref, *rest, rank, rope, quarter):
    if rope:
        cos_ref, sin_ref, ckv_f, ckv_b, kpe_f, kpe_b = rest
    else:
        ckv_f, ckv_b, kpe_f, kpe_b = rest
    acc = jnp.dot(x_ref[...], w_ref[...], preferred_element_type=F32)
    ckv = _rms(acc[:, :rank], g_ref[...])
    kpe = acc[:, rank:rank + LANES]
    if rope:
        kpe = _rope(kpe, cos_ref[...], sin_ref[...], quarter)
    ckv_f[...] = ckv
    ckv_b[...] = ckv.astype(BF16)
    kpe_f[...] = kpe
    kpe_b[...] = kpe.astype(BF16)


def _mm_ckv(x, w, gain, rope, rows_per_batch, rank, rope_dim, name, bm=1024):
    M, K = x.shape
    N = w.shape[1]
    bm = _blk(M if rope is None else rows_per_batch, bm)
    rargs, rspecs = _rope_specs(rope, bm, rows_per_batch)
    kern = functools.partial(_mm_ckv_kernel, rank=rank, rope=rope is not None, quarter=rope_dim // 4)
    in_specs = [pl.BlockSpec((bm, K), lambda i, j: (i, 0)),
                pl.BlockSpec((K, N), lambda i, j: (0, 0)),
                pl.BlockSpec((1, rank), lambda i, j: (0, 0))] + rspecs
    out_shape = [jax.ShapeDtypeStruct((M, rank), F32), jax.ShapeDtypeStruct((M, rank), BF16),
                 jax.ShapeDtypeStruct((M, LANES), F32), jax.ShapeDtypeStruct((M, LANES), BF16)]
    out_specs = [pl.BlockSpec((bm, rank), lambda i, j: (i, 0)), pl.BlockSpec((bm, rank), lambda i, j: (i, 0)),
                 pl.BlockSpec((bm, LANES), lambda i, j: (i, 0)), pl.BlockSpec((bm, LANES), lambda i, j: (i, 0))]
    return pl.pallas_call(
        kern, grid=(M // bm, 1), in_specs=in_specs, out_specs=out_specs, out_shape=out_shape,
        compiler_params=_cparams(2), name=name,
    )(x, w, gain.reshape(1, rank), *rargs)


def _mm_uq_kernel(x_ref, w_ref, *rest, rope, quarter):
    if rope:
        cos_ref, sin_ref, o_ref = rest
    else:
        (o_ref,) = rest
    acc = jnp.dot(x_ref[...], w_ref[...], preferred_element_type=F32)
    for h in range(acc.shape[1] // LANES):
        a = acc[:, h * LANES:(h + 1) * LANES]
        if rope and h % 2 == 1:
            a = _rope(a, cos_ref[...], sin_ref[...], quarter)
        o_ref[:, h * LANES:(h + 1) * LANES] = a.astype(o_ref.dtype)


def _mm_uq(x, w, rope, rows_per_batch, rope_dim, name, bm=1024, bn=1024):
    M, K = x.shape
    N = w.shape[1]
    bm, bn = _blk(M if rope is None else rows_per_batch, bm), _blk(N, bn)
    rargs, rspecs = _rope_specs(rope, bm, rows_per_batch)
    kern = functools.partial(_mm_uq_kernel, rope=rope is not None, quarter=rope_dim // 4)
    return _mm_call(kern, x, [w], N, bm, bn, rargs, rspecs, [BF16], name)[0]


def _mm_sigmoid_kernel(x_ref, w_ref, o_ref):
    acc = jnp.dot(x_ref[...], w_ref[...], preferred_element_type=F32)
    o_ref[...] = _sigmoid(acc).astype(o_ref.dtype)


def _mm_sigmoid(x, w, out_dtype, name, bm=1024, bn=1024):
    M, K = x.shape
    N = w.shape[1]
    bm, bn = _blk(M, bm), _blk(N, bn)
    return _mm_call(_mm_sigmoid_kernel, x, [w], N, bm, bn, [], [], [out_dtype], name)[0]


def _mm_swiglu_kernel(x_ref, wg_ref, wu_ref, o_ref):
    x = x_ref[...]
    a = jnp.dot(x, wg_ref[...], preferred_element_type=F32)
    b = jnp.dot(x, wu_ref[...], preferred_element_type=F32)
    o_ref[...] = (a * _sigmoid(a) * b).astype(o_ref.dtype)


def _mm_swiglu(x, wg, wu, name, bm=1024, bn=256):
    M, K = x.shape
    N = wg.shape[1]
    bm, bn = _blk(M, bm), _blk(N, bn)
    return _mm_call(_mm_swiglu_kernel, x, [wg, wu], N, bm, bn, [], [], [BF16], name)[0]


def _mm_merge_kernel(ya_ref, yb_ref, yc_ref, wa_ref, wb_ref, wc_ref, ga_ref, gb_ref, gc_ref, o_ref):
    a = jnp.dot(ya_ref[...], wa_ref[...], preferred_element_type=F32)
    b = jnp.dot(yb_ref[...], wb_ref[...], preferred_element_type=F32)
    c = jnp.dot(yc_ref[...], wc_ref[...], preferred_element_type=F32)
    o_ref[...] = (ga_ref[...] * a + gb_ref[...] * b + gc_ref[...] * c).astype(o_ref.dtype)


def _mm_merge(ya, yb, yc, wa, wb, wc, gates, name, bm=512, bn=512):
    M = ya.shape[0]
    D = wa.shape[1]
    bm, bn = _blk(M, bm), _blk(D, bn)
    nb = D // bn
    xs = lambda y: pl.BlockSpec((bm, y.shape[1]), lambda i, j: (i, 0))
    wsp = lambda w: pl.BlockSpec((w.shape[0], bn), lambda i, j: (0, j))
    gsp = lambda t: pl.BlockSpec((bm, bn), lambda i, j: (i, j + t * nb))
    return pl.pallas_call(
        _mm_merge_kernel,
        grid=(M // bm, nb),
        in_specs=[xs(ya), xs(yb), xs(yc), wsp(wa), wsp(wb), wsp(wc), gsp(0), gsp(1), gsp(2)],
        out_specs=pl.BlockSpec((bm, bn), lambda i, j: (i, j)),
        out_shape=jax.ShapeDtypeStruct((M, D), BF16),
        compiler_params=_cparams(2),
        name=name,
    )(ya, yb, yc, wa, wb, wc, gates, gates, gates)


def _sgu_kernel(u_ref, v_ref, lg_ref, lb_ref, ws_ref, bias_ref, o_ref, *, chunk, groups):
    vn = _layer_norm(v_ref[...], lg_ref[...], lb_ref[...]).astype(BF16)
    gd = vn.shape[1] // groups
    for c in range(vn.shape[0] // chunk):
        r0 = c * chunk
        for g in range(groups):
            c0 = g * gd
            s = jnp.dot(ws_ref[g], vn[r0:r0 + chunk, c0:c0 + gd], preferred_element_type=F32)
            s = s + bias_ref[:, c0:c0 + gd]
            o_ref[r0:r0 + chunk, c0:c0 + gd] = (u_ref[r0:r0 + chunk, c0:c0 + gd] * s).astype(o_ref.dtype)


def _sgu(uv, ln_g, ln_b, w_s, bias_full, name):
    M = uv.shape[0]
    A = uv.shape[1] // 2
    G, C, _ = w_s.shape
    bt = _blk(M, 2 * C)
    return pl.pallas_call(
        functools.partial(_sgu_kernel, chunk=C, groups=G),
        grid=(M // bt,),
        in_specs=[
            pl.BlockSpec((bt, A), lambda i: (i, 0)),
            pl.BlockSpec((bt, A), lambda i: (i, 1)),
            pl.BlockSpec((1, A), lambda i: (0, 0)),
            pl.BlockSpec((1, A), lambda i: (0, 0)),
            pl.BlockSpec((G, C, C), lambda i: (0, 0, 0)),
            pl.BlockSpec((C, A), lambda i: (0, 0)),
        ],
        out_specs=pl.BlockSpec((bt, A), lambda i: (i, 0)),
        out_shape=jax.ShapeDtypeStruct((M, A), BF16),
        compiler_params=_cparams(1),
        name=name,
    )(uv, uv, ln_g.reshape(1, A), ln_b.reshape(1, A), w_s, bias_full)


def _softmax_pv(s, v):
    m = jnp.max(s, axis=-1, keepdims=True)
    p = jnp.exp(s - m)
    l = jnp.sum(p, axis=-1, keepdims=True)
    return jnp.dot(p.astype(BF16), v, preferred_element_type=F32) / l


_NT = (((1,), (1,)), ((), ()))


def _gqa_kernel(q_ref, k_ref, v_ref, o_ref, *, group, hd, scale):
    bq = q_ref.shape[0]
    q = q_ref[...]
    q2 = jnp.concatenate([q[:, g * hd:(g + 1) * hd] for g in range(group)], axis=0)
    s = lax.dot_general(q2, k_ref[...], _NT, preferred_element_type=F32) * scale
    o = _softmax_pv(s, v_ref[...])
    for g in range(group):
        o_ref[:, g * hd:(g + 1) * hd] = o[g * bq:(g + 1) * bq].astype(o_ref.dtype)


def _gqa(q, k, v, kv_heads, hd, name, bq=128):
    B, Nq, W = q.shape
    S = k.shape[1]
    group = W // (kv_heads * hd)
    bq = _blk(Nq, bq)
    gw = group * hd
    return pl.pallas_call(
        functools.partial(_gqa_kernel, group=group, hd=hd, scale=hd ** -0.5),
        grid=(B, kv_heads, Nq // bq),
        in_specs=[
            pl.BlockSpec((None, bq, gw), lambda b, n, i: (b, i, n)),
            pl.BlockSpec((None, S, hd), lambda b, n, i: (b, 0, n)),
            pl.BlockSpec((None, S, hd), lambda b, n, i: (b, 0, n)),
        ],
        out_specs=pl.BlockSpec((None, bq, gw), lambda b, n, i: (b, i, n)),
        out_shape=jax.ShapeDtypeStruct((B, Nq, W), BF16),
        compiler_params=_cparams(3),
        name=name,
    )(q, k, v)


def _mla_kernel(q_ref, kn_ref, kpe_ref, v_ref, o_ref, *, scale):
    k = jnp.concatenate([kn_ref[...], kpe_ref[...]], axis=1)
    s = lax.dot_general(q_ref[...], k, _NT, preferred_element_type=F32) * scale
    o_ref[...] = _softmax_pv(s, v_ref[...]).astype(o_ref.dtype)


def _mla(q, kv, kpe, heads, scale, name, bq=512):
    B, Nq, _ = q.shape
    S = kv.shape[1]
    bq = _blk(Nq, bq)
    return pl.pallas_call(
        functools.partial(_mla_kernel, scale=scale),
        grid=(B, heads, Nq // bq),
        in_specs=[
            pl.BlockSpec((None, bq, 2 * LANES), lambda b, h, i: (b, i, h)),
            pl.BlockSpec((None, S, LANES), lambda b, h, i: (b, 0, h)),
            pl.BlockSpec((None, S, LANES), lambda b, h, i: (b, 0, 0)),
            pl.BlockSpec((None, S, LANES), lambda b, h, i: (b, 0, heads + h)),
        ],
        out_specs=pl.BlockSpec((None, bq, LANES), lambda b, h, i: (b, i, h)),
        out_shape=jax.ShapeDtypeStruct((B, Nq, heads * LANES), BF16),
        compiler_params=_cparams(3),
        name=name,
    )(q, kv, kpe, kv)


def _rope_tables(n, dim):
    rows = n // GRID_W
    row = jnp.repeat(jnp.arange(rows, dtype=F32), GRID_W)
    col = jnp.tile(jnp.arange(GRID_W, dtype=F32), rows)
    quarter = dim // 4
    freqs = ROPE_THETA ** (-jnp.arange(quarter, dtype=F32) / quarter)
    ra = row[:, None] * freqs[None, :]
    ca = col[:, None] * freqs[None, :]
    ang = jnp.concatenate([ra, ra, ca, ca], axis=-1)
    sign = jnp.where((jnp.arange(dim) // quarter) % 2 == 0, -1.0, 1.0).astype(F32)
    cos = jnp.cos(ang)
    sin = jnp.sin(ang) * sign[None, :]
    if dim < LANES:
        cos = jnp.pad(cos, ((0, 0), (0, LANES - dim)), constant_values=1.0)
        sin = jnp.pad(sin, ((0, 0), (0, LANES - dim)))
    return cos, sin


def _layer_weights(l, dims, w_in, w_s, b_s, w_uq, w_ukv, w_pa, w_pb, w_pc, w_o, w_gate, w_up, w_down):
    A, qw, kvw, qr, rank, rd, D, mh, nope, vd = dims
    o = [0, A, 2 * A, 2 * A + qw, 2 * A + qw + 2 * kvw, 2 * A + qw + 2 * kvw + qr]
    o.append(o[-1] + rank + rd)
    wi = w_in[l]
    bf = lambda a: a.astype(BF16)
    w = {}
    w['uv'] = bf(wi[:, o[0]:o[2]])
    w['q'] = bf(wi[:, o[2]:o[3]])
    w['kv'] = bf(wi[:, o[3]:o[4]])
    w['cq'] = bf(wi[:, o[4]:o[5]])
    w['ckv'] = bf(jnp.pad(wi[:, o[5]:o[6]], ((0, 0), (0, LANES - rd))))
    w['g'] = bf(wi[:, o[6]:])
    uq = w_uq[l].reshape(qr, mh, nope + rd)
    w['uq'] = bf(jnp.pad(uq, ((0, 0), (0, 0), (0, 2 * LANES - nope - rd))).reshape(qr, mh * 2 * LANES))
    ukv = w_ukv[l].reshape(rank, mh, nope + vd)
    w['ukv'] = bf(jnp.concatenate([ukv[:, :, :nope].reshape(rank, mh * nope),
                                   ukv[:, :, nope:].reshape(rank, mh * vd)], axis=1))
    w['pa'], w['pb'], w['pc'], w['o'] = bf(w_pa[l]), bf(w_pb[l]), bf(w_pc[l]), bf(w_o[l])
    w['gate'], w['up'], w['down'] = bf(w_gate[l]), bf(w_up[l]), bf(w_down[l])
    w['s'] = bf(w_s[l])
    gd = A // w_s.shape[1]
    w['sb'] = jnp.repeat(b_s[l].T, gd, axis=1)
    return w


def _trunk_layer(x, h, mods, w, vecs, dims, B, T, rope, ctx, alpha, nxt):
    A, qw, kvw, qr, rank, rd, D, mh, nope, vd = dims
    sgu_g, sgu_b, qg, kg, cqg, ckvg, ln1g, ln1b, ln2g, ln2b = vecs
    M = B * T
    G = mods.shape[0]
    rpg = M // G
    hd = LANES
    kvh = kvw // hd
    tag = "s" if rope is not None else "p"
    rope_g = None if rope is None else rope[0]
    rope_m = None if rope is None else rope[1]

    uv = _mm_plain(h, w['uv'], F32, "proj_uv_" + tag)
    q = _mm_heads(h, w['q'], qg, rope_g, T, 1 << 30, [BF16], "proj_q_" + tag)[0]
    kv_f, kv_b = _mm_heads(h, w['kv'], kg, rope_g, T, kvh, [F32, BF16], "proj_kv_" + tag)
    cq = _mm_rms(h, w['cq'], cqg, BF16, "proj_cq_" + tag)
    ckv_f, ckv_b, kpe_f, kpe_b = _mm_ckv(h, w['ckv'], ckvg, rope_m, T, rank, rd, "proj_ckv_" + tag)
    gates = _mm_sigmoid(h, w['g'], F32, "proj_gates_" + tag)

    y_a = _sgu(uv, sgu_g, sgu_b, w['s'], w['sb'], "sgu_" + tag)

    k_b = kv_b[:, :kvw].reshape(B, T, kvw)
    v_b = kv_b[:, kvw:].reshape(B, T, kvw)
    if ctx is not None:
        c_k, c_v, c_ckv, c_kpe = ctx
        P = c_k.shape[1]
        k_b = jnp.concatenate([c_k.reshape(B, P, kvw).astype(BF16), k_b], axis=1)
        v_b = jnp.concatenate([c_v.reshape(B, P, kvw).astype(BF16), v_b], axis=1)
    y_b = _gqa(q.reshape(B, T, qw), k_b, v_b, kvh, hd, "gqa_" + tag).reshape(M, qw)

    qc = _mm_uq(cq, w['uq'], rope_m, T, rd, "mla_uq_" + tag)
    ckv_all = ckv_b.reshape(B, T, rank)
    kpe_all = kpe_b.reshape(B, T, LANES)
    if ctx is not None:
        ckv_all = jnp.concatenate([c_ckv.astype(BF16), ckv_all], axis=1)
        c_kpe_pad = jnp.pad(c_kpe, ((0, 0), (0, 0), (0, LANES - rd))).astype(BF16)
        kpe_all = jnp.concatenate([c_kpe_pad, kpe_all], axis=1)
    S = ckv_all.shape[1]
    kvu = _mm_plain(ckv_all.reshape(B * S, rank), w['ukv'], BF16, "mla_ukv_" + tag)
    y_c = _mla(qc.reshape(B, T, mh * 2 * LANES), kvu.reshape(B, S, mh * (nope + vd)), kpe_all,
               mh, (nope + rd) ** -0.5, "mla_" + tag).reshape(M, mh * vd)

    merged = _mm_merge(y_a, y_b, y_c, w['pa'], w['pb'], w['pc'], gates, "merge_" + tag)
    mix = _mm_plain(merged, w['o'], F32, "proj_o_" + tag)
    x1, h2 = _ln_residual(x, mix, mods, 2, ln1g, ln1b, alpha, rpg, nxt=(mods, 4, 3))
    hid = _mm_swiglu(h2, w['gate'], w['up'], "ffn_in_" + tag)
    ff = _mm_plain(hid, w['down'], F32, "ffn_out_" + tag, bm=512, bn=512)
    x2, h_next = _ln_residual(x1, ff, mods, 5, ln2g, ln2b, alpha, rpg, nxt=nxt)

    own = (kv_f[:, :kvw], kv_f[:, kvw:], ckv_f, kpe_f[:, :rd])
    return x2, h_next, own


def kernel(x_prompt, x_sample, cache_k, cache_v, cache_ckv, cache_kpe, c, c_ctx,
           w_ada, b_ada, w_in, sgu_ln_g, sgu_ln_b, w_s, b_s, q_norm_g, k_norm_g,
           mla_q_norm_g, mla_kv_norm_g, w_uq, w_ukv, w_pa, w_pb, w_pc, w_o,
           ln1_g, ln1_b, ln2_g, ln2_b, w_gate, w_up, w_down):
    Bp, Tp, D = x_prompt.shape
    Bs, Ts, _ = x_sample.shape
    L = w_ada.shape[0]
    A = sgu_ln_g.shape[1]
    hd = q_norm_g.shape[1]
    kvh = cache_k.shape[3]
    qw = w_pb.shape[1]
    kvw = kvh * hd
    qr = mla_q_norm_g.shape[1]
    rank = mla_kv_norm_g.shape[1]
    rd = cache_kpe.shape[-1]
    vd = LANES
    mh = w_pc.shape[1] // vd
    nope = w_uq.shape[2] // mh - rd
    assert hd == LANES and nope == LANES and w_ukv.shape[2] == mh * (nope + vd)
    dims = (A, qw, kvw, qr, rank, rd, D, mh, nope, vd)
    alpha = float((2 * L) ** 0.25)

    R = -(-(1 + Bs) // 8) * 8
    cond = jnp.concatenate([c_ctx[None, :], c, jnp.zeros((R - 1 - Bs, D), F32)], axis=0)
    mods_all = _ada(cond, w_ada, b_ada)

    rope = (_rope_tables(Ts, hd), _rope_tables(Ts, rd))

    xp = x_prompt.reshape(Bp * Tp, D)
    xs = x_sample.reshape(Bs * Ts, D)
    mods = [(mods_all[l, 0:1].reshape(1, 1, 6 * D), mods_all[l, 1:1 + Bs].reshape(Bs, 1, 6 * D))
            for l in range(L)]
    hp = _modulate(xp, mods[0][0], 1, 0, Bp * Tp)
    hs = _modulate(xs, mods[0][1], 1, 0, Ts)

    new_k, new_v, new_ckv, new_kpe = [], [], [], []
    for l in range(L):
        w = _layer_weights(l, dims, w_in, w_s, b_s, w_uq, w_ukv, w_pa, w_pb, w_pc, w_o, w_gate, w_up, w_down)
        vecs = (sgu_ln_g[l], sgu_ln_b[l], q_norm_g[l], k_norm_g[l], mla_q_norm_g[l], mla_kv_norm_g[l],
                ln1_g[l], ln1_b[l], ln2_g[l], ln2_b[l])
        nxt_p = (mods[l + 1][0], 1, 0) if l + 1 < L else None
        nxt_s = (mods[l + 1][1], 1, 0) if l + 1 < L else None
        xp, hp, own = _trunk_layer(xp, hp, mods[l][0], w, vecs, dims, Bp, Tp, None, None, alpha, nxt_p)
        new_k.append(own[0].reshape(Bp, Tp, kvh, hd))
        new_v.append(own[1].reshape(Bp, Tp, kvh, hd))
        new_ckv.append(own[2].reshape(Bp, Tp, rank))
        new_kpe.append(own[3].reshape(Bp, Tp, rd))
        ctx = (cache_k[:, l], cache_v[:, l], cache_ckv[:, l], cache_kpe[:, l])
        xs, hs, _ = _trunk_layer(xs, hs, mods[l][1], w, vecs, dims, Bs, Ts, rope, ctx, alpha, nxt_s)

    return (xp.reshape(Bp, Tp, D), xs.reshape(Bs, Ts, D),
            jnp.stack(new_k, axis=1), jnp.stack(new_v, axis=1),
            jnp.stack(new_ckv, axis=1), jnp.stack(new_kpe, axis=1))
```

```python
import functools

import jax
import jax.numpy as jnp
from jax import lax
from jax.experimental import pallas as pl
from jax.experimental.pallas import tpu as pltpu

F32 = jnp.float32
BF16 = jnp.bfloat16

NORM_EPS = 1e-6
ROPE_THETA = 10000.0
GRID_W = 64

LANES = 128
MXU_DIM = 256
VMEM_LIMIT_BYTES = 56 * 1024 * 1024


def _cparams(n_axes):
    return pltpu.CompilerParams(
        dimension_semantics=("arbitrary",) * n_axes,
        vmem_limit_bytes=VMEM_LIMIT_BYTES,
    )


def _blk(n, pref):
    b = min(n, pref)
    while n % b:
        b //= 2
    return b


def _rms(a, g):
    ms = jnp.mean(a * a, axis=-1, keepdims=True)
    return a * lax.rsqrt(ms + NORM_EPS) * g


def _rope(y, cos, sin_signed, quarter):
    n = y.shape[-1]
    lane = lax.broadcasted_iota(jnp.int32, y.shape, 1)
    first = (lane & quarter) == 0
    rot = jnp.where(first, pltpu.roll(y, n - quarter, axis=1), pltpu.roll(y, quarter, axis=1))
    return y * cos + rot * sin_signed


def _sigmoid(x):
    return 1.0 / (1.0 + jnp.exp(-x))


def _layer_norm(z, g, b):
    mu = jnp.mean(z, axis=-1, keepdims=True)
    zc = z - mu
    var = jnp.mean(zc * zc, axis=-1, keepdims=True)
    return zc * lax.rsqrt(var + NORM_EPS) * g + b


def _ada_kernel(c_ref, w_ref, b_ref, o_ref):
    c = c_ref[...]
    s = (c * _sigmoid(c)).astype(BF16)
    o_ref[...] = jnp.dot(s, w_ref[...].astype(BF16), preferred_element_type=F32) + b_ref[...]


def _ada(cond, w_ada, b_ada):
    L, D, N = w_ada.shape
    R = cond.shape[0]
    bn = _blk(N, 512)
    return pl.pallas_call(
        _ada_kernel,
        grid=(L, N // bn),
        in_specs=[
            pl.BlockSpec((R, D), lambda l, j: (0, 0)),
            pl.BlockSpec((None, D, bn), lambda l, j: (l, 0, j)),
            pl.BlockSpec((None, 1, bn), lambda l, j: (l, 0, j)),
        ],
        out_specs=pl.BlockSpec((None, R, bn), lambda l, j: (l, 0, j)),
        out_shape=jax.ShapeDtypeStruct((L, R, N), F32),
        compiler_params=_cparams(2),
        name="ada_mod",
    )(cond, w_ada, b_ada.reshape(L, 1, N))


def _modulate_kernel(x_ref, sc_ref, sh_ref, h_ref):
    h_ref[...] = (x_ref[...] * (1.0 + sc_ref[...]) + sh_ref[...]).astype(h_ref.dtype)


def _mod_spec(D, k, rows_per_group, bt):
    return pl.BlockSpec((None, 1, D), lambda i, *_: ((i * bt) // rows_per_group, 0, k))


def _modulate(x, mods, k_sc, k_sh, rows_per_group):
    M, D = x.shape
    bt = _blk(rows_per_group, 512)
    return pl.pallas_call(
        _modulate_kernel,
        grid=(M // bt,),
        in_specs=[
            pl.BlockSpec((bt, D), lambda i: (i, 0)),
            _mod_spec(D, k_sc, rows_per_group, bt),
            _mod_spec(D, k_sh, rows_per_group, bt),
        ],
        out_specs=pl.BlockSpec((bt, D), lambda i: (i, 0)),
        out_shape=jax.ShapeDtypeStruct((M, D), BF16),
        compiler_params=_cparams(1),
        name="modulate",
    )(x, mods, mods)


def _ln_kernel(x_ref, y_ref, gate_ref, g_ref, b_ref, *rest, alpha, with_h):
    z = alpha * x_ref[...] + gate_ref[...] * y_ref[...]
    xn = _layer_norm(z, g_ref[...], b_ref[...])
    if with_h:
        sc_ref, sh_ref, xo_ref, h_ref = rest
        xo_ref[...] = xn
        h_ref[...] = (xn * (1.0 + sc_ref[...]) + sh_ref[...]).astype(h_ref.dtype)
    else:
        (xo_ref,) = rest
        xo_ref[...] = xn


def _ln_residual(x, y, mods, k_gate, ln_g, ln_b, alpha, rows_per_group, nxt=None):
    M, D = x.shape
    bt = _blk(rows_per_group, 256)
    row = pl.BlockSpec((bt, D), lambda i: (i, 0))
    vec = pl.BlockSpec((1, D), lambda i: (0, 0))
    in_specs = [row, row, _mod_spec(D, k_gate, rows_per_group, bt), vec, vec]
    args = [x, y, mods, ln_g.reshape(1, D), ln_b.reshape(1, D)]
    out_shape = [jax.ShapeDtypeStruct((M, D), F32)]
    out_specs = [row]
    if nxt is not None:
        mods_n, k_sc, k_sh = nxt
        in_specs += [_mod_spec(D, k_sc, rows_per_group, bt), _mod_spec(D, k_sh, rows_per_group, bt)]
        args += [mods_n, mods_n]
        out_shape.append(jax.ShapeDtypeStruct((M, D), BF16))
        out_specs.append(row)
    res = pl.pallas_call(
        functools.partial(_ln_kernel, alpha=alpha, with_h=nxt is not None),
        grid=(M // bt,),
        in_specs=in_specs,
        out_specs=out_specs,
        out_shape=out_shape,
        compiler_params=_cparams(1),
        name="ln_residual",
    )(*args)
    return res if nxt is not None else (res[0], None)


def _mm_call(kernel, x, ws, n_cols, bm, bn, extras, extra_specs, out_dtypes, name, out_cols=None):
    M, K = x.shape
    out_cols = n_cols if out_cols is None else out_cols
    in_specs = [pl.BlockSpec((bm, K), lambda i, j: (i, 0))]
    in_specs += [pl.BlockSpec((w.shape[0], bn), lambda i, j: (0, j)) for w in ws]
    in_specs += list(extra_specs)
    out_shape = [jax.ShapeDtypeStruct((M, out_cols), dt) for dt in out_dtypes]
    out_specs = [pl.BlockSpec((bm, bn), lambda i, j: (i, j)) for _ in out_dtypes]
    return pl.pallas_call(
        kernel,
        grid=(M // bm, n_cols // bn),
        in_specs=in_specs,
        out_specs=out_specs,
        out_shape=out_shape,
        compiler_params=_cparams(2),
        name=name,
    )(x, *ws, *extras)


def _mm_plain_kernel(x_ref, w_ref, o_ref):
    o_ref[...] = jnp.dot(x_ref[...], w_ref[...], preferred_element_type=F32).astype(o_ref.dtype)


def _mm_plain(x, w, out_dtype, name, bm=1024, bn=1024):
    M, K = x.shape
    N = w.shape[1]
    bm, bn = _blk(M, bm), _blk(N, bn)
    return _mm_call(_mm_plain_kernel, x, [w], N, bm, bn, [], [], [out_dtype], name)[0]


def _rope_specs(rope, bm, rows_per_batch):
    if rope is None:
        return [], []
    nb = rows_per_batch // bm
    spec = pl.BlockSpec((bm, LANES), lambda i, j: (i % nb, 0))
    return [rope[0], rope[1]], [spec, spec]


def _mm_heads_kernel(x_ref, w_ref, g_ref, *rest, n_norm, rope, quarter):
    if rope:
        cos_ref, sin_ref, *outs = rest
    else:
        outs = rest
    acc = jnp.dot(x_ref[...], w_ref[...], preferred_element_type=F32)
    for h in range(acc.shape[1] // LANES):
        a = acc[:, h * LANES:(h + 1) * LANES]
        if h < n_norm:
            a = _rms(a, g_ref[...])
            if rope:
                a = _rope(a, cos_ref[...], sin_ref[...], quarter)
        for o in outs:
            o[:, h * LANES:(h + 1) * LANES] = a.astype(o.dtype)


def _mm_heads(x, w, gain, rope, rows_per_batch, n_norm, out_dtypes, name, bm=1024, bn=1024):
    M, K = x.shape
    N = w.shape[1]
    bm, bn = _blk(M if rope is None else rows_per_batch, bm), _blk(N, bn)
    rargs, rspecs = _rope_specs(rope, bm, rows_per_batch)
    kern = functools.partial(_mm_heads_kernel, n_norm=n_norm, rope=rope is not None,
                             quarter=LANES // 4)
    return _mm_call(kern, x, [w], N, bm, bn,
                    [gain.reshape(1, LANES)] + rargs,
                    [pl.BlockSpec((1, LANES), lambda i, j: (0, 0))] + rspecs,
                    out_dtypes, name)


def _mm_rms_kernel(x_ref, w_ref, g_ref, o_ref):
    acc = jnp.dot(x_ref[...], w_ref[...], preferred_element_type=F32)
    o_ref[...] = _rms(acc, g_ref[...]).astype(o_ref.dtype)


def _mm_rms(x, w, gain, out_dtype, name, bm=1024):
    M, K = x.shape
    N = w.shape[1]
    bm = _blk(M, bm)
    return _mm_call(_mm_rms_kernel, x, [w], N, bm, N, [gain.reshape(1, N)],
                    [pl.BlockSpec((1, N), lambda i, j: (0, 0))], [out_dtype], name)[0]


def _mm_ckv_kernel(x_ref, w_ref, g_ref, *rest, rank, rope, quarter):
    if rope:
        cos_ref, sin_ref, ckv_f, ckv_b, kpe_f, kpe_b = rest
    else:
        ckv_f, ckv_b, kpe_f, kpe_b = rest
    acc = jnp.dot(x_ref[...], w_ref[...], preferred_element_type=F32)
    ckv = _rms(acc[:, :rank], g_ref[...])
    kpe = acc[:, rank:rank + LANES]
    if rope:
        kpe = _rope(kpe, cos_ref[...], sin_ref[...], quarter)
    ckv_f[...] = ckv
    ckv_b[...] = ckv.astype(BF16)
    kpe_f[...] = kpe
    kpe_b[...] = kpe.astype(BF16)


def _mm_ckv(x, w, gain, rope, rows_per_batch, rank, rope_dim, name, bm=1024):
    M, K = x.shape
    N = w.shape[1]
    bm = _blk(M if rope is None else rows_per_batch, bm)
    rargs, rspecs = _rope_specs(rope, bm, rows_per_batch)
    kern = functools.partial(_mm_ckv_kernel, rank=rank, rope=rope is not None, quarter=rope_dim // 4)
    in_specs = [pl.BlockSpec((bm, K), lambda i, j: (i, 0)),
                pl.BlockSpec((K, N), lambda i, j: (0, 0)),
                pl.BlockSpec((1, rank), lambda i, j: (0, 0))] + rspecs
    out_shape = [jax.ShapeDtypeStruct((M, rank), F32), jax.ShapeDtypeStruct((M, rank), BF16),
                 jax.ShapeDtypeStruct((M, LANES), F32), jax.ShapeDtypeStruct((M, LANES), BF16)]
    out_specs = [pl.BlockSpec((bm, rank), lambda i, j: (i, 0)), pl.BlockSpec((bm, rank), lambda i, j: (i, 0)),
                 pl.BlockSpec((bm, LANES), lambda i, j: (i, 0)), pl.BlockSpec((bm, LANES), lambda i, j: (i, 0))]
    return pl.pallas_call(
        kern, grid=(M // bm, 1), in_specs=in_specs, out_specs=out_specs, out_shape=out_shape,
        compiler_params=_cparams(2), name=name,
    )(x, w, gain.reshape(1, rank), *rargs)


def _mm_uq_kernel(x_ref, w_ref, *rest, rope, quarter):
    if rope:
        cos_ref, sin_ref, o_ref = rest
    else:
        (o_ref,) = rest
    acc = jnp.dot(x_ref[...], w_ref[...], preferred_element_type=F32)
    for h in range(acc.shape[1] // LANES):
        a = acc[:, h * LANES:(h + 1) * LANES]
        if rope and h % 2 == 1:
            a = _rope(a, cos_ref[...], sin_ref[...], quarter)
        o_ref[:, h * LANES:(h + 1) * LANES] = a.astype(o_ref.dtype)


def _mm_uq(x, w, rope, rows_per_batch, rope_dim, name, bm=1024, bn=1024):
    M, K = x.shape
    N = w.shape[1]
    bm, bn = _blk(M if rope is None else rows_per_batch, bm), _blk(N, bn)
    rargs, rspecs = _rope_specs(rope, bm, rows_per_batch)
    kern = functools.partial(_mm_uq_kernel, rope=rope is not None, quarter=rope_dim // 4)
    return _mm_call(kern, x, [w], N, bm, bn, rargs, rspecs, [BF16], name)[0]


def _mm_sigmoid_kernel(x_ref, w_ref, o_ref):
    acc = jnp.dot(x_ref[...], w_ref[...], preferred_element_type=F32)
    o_ref[...] = _sigmoid(acc).astype(o_ref.dtype)


def _mm_sigmoid(x, w, out_dtype, name, bm=1024, bn=1024):
    M, K = x.shape
    N = w.shape[1]
    bm, bn = _blk(M, bm), _blk(N, bn)
    return _mm_call(_mm_sigmoid_kernel, x, [w], N, bm, bn, [], [], [out_dtype], name)[0]


def _mm_swiglu_kernel(x_ref, wg_ref, wu_ref, o_ref):
    x = x_ref[...]
    a = jnp.dot(x, wg_ref[...], preferred_element_type=F32)
    b = jnp.dot(x, wu_ref[...], preferred_element_type=F32)
    o_ref[...] = (a * _sigmoid(a) * b).astype(o_ref.dtype)


def _mm_swiglu(x, wg, wu, name, bm=1024, bn=256):
    M, K = x.shape
    N = wg.shape[1]
    bm, bn = _blk(M, bm), _blk(N, bn)
    return _mm_call(_mm_swiglu_kernel, x, [wg, wu], N, bm, bn, [], [], [BF16], name)[0]


def _mm_merge_kernel(ya_ref, yb_ref, yc_ref, wa_ref, wb_ref, wc_ref, ga_ref, gb_ref, gc_ref, o_ref):
    a = jnp.dot(ya_ref[...], wa_ref[...], preferred_element_type=F32)
    b = jnp.dot(yb_ref[...], wb_ref[...], preferred_element_type=F32)
    c = jnp.dot(yc_ref[...], wc_ref[...], preferred_element_type=F32)
    o_ref[...] = (ga_ref[...] * a + gb_ref[...] * b + gc_ref[...] * c).astype(o_ref.dtype)


def _mm_merge(ya, yb, yc, wa, wb, wc, gates, name, bm=512, bn=512):
    M = ya.shape[0]
    D = wa.shape[1]
    bm, bn = _blk(M, bm), _blk(D, bn)
    nb = D // bn
    xs = lambda y: pl.BlockSpec((bm, y.shape[1]), lambda i, j: (i, 0))
    wsp = lambda w: pl.BlockSpec((w.shape[0], bn), lambda i, j: (0, j))
    gsp = lambda t: pl.BlockSpec((bm, bn), lambda i, j: (i, j + t * nb))
    return pl.pallas_call(
        _mm_merge_kernel,
        grid=(M // bm, nb),
        in_specs=[xs(ya), xs(yb), xs(yc), wsp(wa), wsp(wb), wsp(wc), gsp(0), gsp(1), gsp(2)],
        out_specs=pl.BlockSpec((bm, bn), lambda i, j: (i, j)),
        out_shape=jax.ShapeDtypeStruct((M, D), BF16),
        compiler_params=_cparams(2),
        name=name,
    )(ya, yb, yc, wa, wb, wc, gates, gates, gates)


def _sgu_kernel(u_ref, v_ref, lg_ref, lb_ref, ws_ref, bias_ref, o_ref, *, chunk, groups):
    vn = _layer_norm(v_ref[...], lg_ref[...], lb_ref[...]).astype(BF16)
    gd = vn.shape[1] // groups
    for c in range(vn.shape[0] // chunk):
        r0 = c * chunk
        for g in range(groups):
            c0 = g * gd
            s = jnp.dot(ws_ref[g], vn[r0:r0 + chunk, c0:c0 + gd], preferred_element_type=F32)
            s = s + bias_ref[:, c0:c0 + gd]
            o_ref[r0:r0 + chunk, c0:c0 + gd] = (u_ref[r0:r0 + chunk, c0:c0 + gd] * s).astype(o_ref.dtype)


def _sgu(uv, ln_g, ln_b, w_s, bias_full, name):
    M = uv.shape[0]
    A = uv.shape[1] // 2
    G, C, _ = w_s.shape
    bt = _blk(M, 2 * C)
    return pl.pallas_call(
        functools.partial(_sgu_kernel, chunk=C, groups=G),
        grid=(M // bt,),
        in_specs=[
            pl.BlockSpec((bt, A), lambda i: (i, 0)),
            pl.BlockSpec((bt, A), lambda i: (i, 1)),
            pl.BlockSpec((1, A), lambda i: (0, 0)),
            pl.BlockSpec((1, A), lambda i: (0, 0)),
            pl.BlockSpec((G, C, C), lambda i: (0, 0, 0)),
            pl.BlockSpec((C, A), lambda i: (0, 0)),
        ],
        out_specs=pl.BlockSpec((bt, A), lambda i: (i, 0)),
        out_shape=jax.ShapeDtypeStruct((M, A), BF16),
        compiler_params=_cparams(1),
        name=name,
    )(uv, uv, ln_g.reshape(1, A), ln_b.reshape(1, A), w_s, bias_full)


_NT = (((1,), (1,)), ((), ()))
_LOG2E = 1.4426950408889634


def _attend(q, k, v, scale):
    s = lax.dot_general(q, k, _NT, preferred_element_type=F32)
    m = jnp.max(s, axis=-1, keepdims=True)
    p = jnp.exp2((s - m) * (scale * _LOG2E))
    l = jnp.sum(p, axis=-1, keepdims=True)
    return jnp.dot(p.astype(BF16), v, preferred_element_type=F32) / l


def _gqa_kernel(q_ref, k_ref, v_ref, o_ref, *, group, hd, scale, sub):
    k = k_ref[...]
    v = v_ref[...]
    for g in range(group):
        for r in range(0, q_ref.shape[0], sub):
            o = _attend(q_ref[r:r + sub, g * hd:(g + 1) * hd], k, v, scale)
            o_ref[r:r + sub, g * hd:(g + 1) * hd] = o.astype(o_ref.dtype)


def _gqa(q, k, v, kv_heads, hd, name, bq=1024, sub=512):
    B, Nq, W = q.shape
    S = k.shape[1]
    group = W // (kv_heads * hd)
    bq = _blk(Nq, bq)
    gw = group * hd
    sub = _blk(bq, sub)
    return pl.pallas_call(
        functools.partial(_gqa_kernel, group=group, hd=hd, scale=hd ** -0.5, sub=sub),
        grid=(B, kv_heads, Nq // bq),
        in_specs=[
            pl.BlockSpec((None, bq, gw), lambda b, n, i: (b, i, n)),
            pl.BlockSpec((None, S, hd), lambda b, n, i: (b, 0, n)),
            pl.BlockSpec((None, S, hd), lambda b, n, i: (b, 0, n)),
        ],
        out_specs=pl.BlockSpec((None, bq, gw), lambda b, n, i: (b, i, n)),
        out_shape=jax.ShapeDtypeStruct((B, Nq, W), BF16),
        compiler_params=_cparams(3),
        name=name,
    )(q, k, v)


def _mla_kernel(q_ref, kn_ref, kpe_ref, v_ref, o_ref, *, scale, sub):
    k = jnp.concatenate([kn_ref[...], kpe_ref[...]], axis=1)
    v = v_ref[...]
    for r in range(0, q_ref.shape[0], sub):
        o = _attend(q_ref[r:r + sub, :], k, v, scale)
        o_ref[r:r + sub, :] = o.astype(o_ref.dtype)


def _mla(q, kv, kpe, heads, scale, name, bq=4096, sub=512):
    B, Nq, _ = q.shape
    S = kv.shape[1]
    bq = _blk(Nq, bq)
    sub = _blk(bq, sub)
    return pl.pallas_call(
        functools.partial(_mla_kernel, scale=scale, sub=sub),
        grid=(B, heads, Nq // bq),
        in_specs=[
            pl.BlockSpec((None, bq, 2 * LANES), lambda b, h, i: (b, i, h)),
            pl.BlockSpec((None, S, LANES), lambda b, h, i: (b, 0, h)),
            pl.BlockSpec((None, S, LANES), lambda b, h, i: (b, 0, 0)),
            pl.BlockSpec((None, S, LANES), lambda b, h, i: (b, 0, heads + h)),
        ],
        out_specs=pl.BlockSpec((None, bq, LANES), lambda b, h, i: (b, i, h)),
        out_shape=jax.ShapeDtypeStruct((B, Nq, heads * LANES), BF16),
        compiler_params=_cparams(3),
        name=name,
    )(q, kv, kpe, kv)


def _rope_tables(n, dim):
    rows = n // GRID_W
    row = jnp.repeat(jnp.arange(rows, dtype=F32), GRID_W)
    col = jnp.tile(jnp.arange(GRID_W, dtype=F32), rows)
    quarter = dim // 4
    freqs = ROPE_THETA ** (-jnp.arange(quarter, dtype=F32) / quarter)
    ra = row[:, None] * freqs[None, :]
    ca = col[:, None] * freqs[None, :]
    ang = jnp.concatenate([ra, ra, ca, ca], axis=-1)
    sign = jnp.where((jnp.arange(dim) // quarter) % 2 == 0, -1.0, 1.0).astype(F32)
    cos = jnp.cos(ang)
    sin = jnp.sin(ang) * sign[None, :]
    if dim < LANES:
        cos = jnp.pad(cos, ((0, 0), (0, LANES - dim)), constant_values=1.0)
        sin = jnp.pad(sin, ((0, 0), (0, LANES - dim)))
    return cos, sin


def _layer_weights(l, dims, w_in, w_s, b_s, w_uq, w_ukv, w_pa, w_pb, w_pc, w_o, w_gate, w_up, w_down):
    A, qw, kvw, qr, rank, rd, D, mh, nope, vd = dims
    o = [0, A, 2 * A, 2 * A + qw, 2 * A + qw + 2 * kvw, 2 * A + qw + 2 * kvw + qr]
    o.append(o[-1] + rank + rd)
    wi = w_in[l]
    bf = lambda a: a.astype(BF16)
    w = {}
    w['uv'] = bf(wi[:, o[0]:o[2]])
    w['q'] = bf(wi[:, o[2]:o[3]])
    w['kv'] = bf(wi[:, o[3]:o[4]])
    w['cq'] = bf(wi[:, o[4]:o[5]])
    w['ckv'] = bf(jnp.pad(wi[:, o[5]:o[6]], ((0, 0), (0, LANES - rd))))
    w['g'] = bf(wi[:, o[6]:])
    uq = w_uq[l].reshape(qr, mh, nope + rd)
    w['uq'] = bf(jnp.pad(uq, ((0, 0), (0, 0), (0, 2 * LANES - nope - rd))).reshape(qr, mh * 2 * LANES))
    ukv = w_ukv[l].reshape(rank, mh, nope + vd)
    w['ukv'] = bf(jnp.concatenate([ukv[:, :, :nope].reshape(rank, mh * nope),
                                   ukv[:, :, nope:].reshape(rank, mh * vd)], axis=1))
    w['pa'], w['pb'], w['pc'], w['o'] = bf(w_pa[l]), bf(w_pb[l]), bf(w_pc[l]), bf(w_o[l])
    w['gate'], w['up'], w['down'] = bf(w_gate[l]), bf(w_up[l]), bf(w_down[l])
    w['s'] = bf(w_s[l])
    gd = A // w_s.shape[1]
    w['sb'] = jnp.repeat(b_s[l].T, gd, axis=1)
    return w


def _trunk_layer(x, h, mods, w, vecs, dims, B, T, rope, ctx, alpha, nxt):
    A, qw, kvw, qr, rank, rd, D, mh, nope, vd = dims
    sgu_g, sgu_b, qg, kg, cqg, ckvg, ln1g, ln1b, ln2g, ln2b = vecs
    M = B * T
    G = mods.shape[0]
    rpg = M // G
    hd = LANES
    kvh = kvw // hd
    tag = "s" if rope is not None else "p"
    rope_g = None if rope is None else rope[0]
    rope_m = None if rope is None else rope[1]

    uv = _mm_plain(h, w['uv'], F32, "proj_uv_" + tag)
    q = _mm_heads(h, w['q'], qg, rope_g, T, 1 << 30, [BF16], "proj_q_" + tag)[0]
    kv_f, kv_b = _mm_heads(h, w['kv'], kg, rope_g, T, kvh, [F32, BF16], "proj_kv_" + tag)
    cq = _mm_rms(h, w['cq'], cqg, BF16, "proj_cq_" + tag)
    ckv_f, ckv_b, kpe_f, kpe_b = _mm_ckv(h, w['ckv'], ckvg, rope_m, T, rank, rd, "proj_ckv_" + tag)
    gates = _mm_sigmoid(h, w['g'], F32, "proj_gates_" + tag)

    y_a = _sgu(uv, sgu_g, sgu_b, w['s'], w['sb'], "sgu_" + tag)

    k_b = kv_b[:, :kvw].reshape(B, T, kvw)
    v_b = kv_b[:, kvw:].reshape(B, T, kvw)
    if ctx is not None:
        c_k, c_v, c_ckv, c_kpe = ctx
        P = c_k.shape[1]
        k_b = jnp.concatenate([c_k.reshape(B, P, kvw).astype(BF16), k_b], axis=1)
        v_b = jnp.concatenate([c_v.reshape(B, P, kvw).astype(BF16), v_b], axis=1)
    y_b = _gqa(q.reshape(B, T, qw), k_b, v_b, kvh, hd, "gqa_" + tag).reshape(M, qw)

    qc = _mm_uq(cq, w['uq'], rope_m, T, rd, "mla_uq_" + tag)
    ckv_all = ckv_b.reshape(B, T, rank)
    kpe_all = kpe_b.reshape(B, T, LANES)
    if ctx is not None:
        ckv_all = jnp.concatenate([c_ckv.astype(BF16), ckv_all], axis=1)
        c_kpe_pad = jnp.pad(c_kpe, ((0, 0), (0, 0), (0, LANES - rd))).astype(BF16)
        kpe_all = jnp.concatenate([c_kpe_pad, kpe_all], axis=1)
    S = ckv_all.shape[1]
    kvu = _mm_plain(ckv_all.reshape(B * S, rank), w['ukv'], BF16, "mla_ukv_" + tag)
    y_c = _mla(qc.reshape(B, T, mh * 2 * LANES), kvu.reshape(B, S, mh * (nope + vd)), kpe_all,
               mh, (nope + rd) ** -0.5, "mla_" + tag).reshape(M, mh * vd)

    merged = _mm_merge(y_a, y_b, y_c, w['pa'], w['pb'], w['pc'], gates, "merge_" + tag)
    mix = _mm_plain(merged, w['o'], F32, "proj_o_" + tag)
    x1, h2 = _ln_residual(x, mix, mods, 2, ln1g, ln1b, alpha, rpg, nxt=(mods, 4, 3))
    hid = _mm_swiglu(h2, w['gate'], w['up'], "ffn_in_" + tag)
    ff = _mm_plain(hid, w['down'], F32, "ffn_out_" + tag, bm=512, bn=512)
    x2, h_next = _ln_residual(x1, ff, mods, 5, ln2g, ln2b, alpha, rpg, nxt=nxt)

    own = (kv_f[:, :kvw], kv_f[:, kvw:], ckv_f, kpe_f[:, :rd])
    return x2, h_next, own


def kernel(x_prompt, x_sample, cache_k, cache_v, cache_ckv, cache_kpe, c, c_ctx,
           w_ada, b_ada, w_in, sgu_ln_g, sgu_ln_b, w_s, b_s, q_norm_g, k_norm_g,
           mla_q_norm_g, mla_kv_norm_g, w_uq, w_ukv, w_pa, w_pb, w_pc, w_o,
           ln1_g, ln1_b, ln2_g, ln2_b, w_gate, w_up, w_down):
    Bp, Tp, D = x_prompt.shape
    Bs, Ts, _ = x_sample.shape
    L = w_ada.shape[0]
    A = sgu_ln_g.shape[1]
    hd = q_norm_g.shape[1]
    kvh = cache_k.shape[3]
    qw = w_pb.shape[1]
    kvw = kvh * hd
    qr = mla_q_norm_g.shape[1]
    rank = mla_kv_norm_g.shape[1]
    rd = cache_kpe.shape[-1]
    vd = LANES
    mh = w_pc.shape[1] // vd
    nope = w_uq.shape[2] // mh - rd
    assert hd == LANES and nope == LANES and w_ukv.shape[2] == mh * (nope + vd)
    dims = (A, qw, kvw, qr, rank, rd, D, mh, nope, vd)
    alpha = float((2 * L) ** 0.25)

    R = -(-(1 + Bs) // 8) * 8
    cond = jnp.concatenate([c_ctx[None, :], c, jnp.zeros((R - 1 - Bs, D), F32)], axis=0)
    mods_all = _ada(cond, w_ada, b_ada)

    rope = (_rope_tables(Ts, hd), _rope_tables(Ts, rd))

    xp = x_prompt.reshape(Bp * Tp, D)
    xs = x_sample.reshape(Bs * Ts, D)
    mods = [(mods_all[l, 0:1].reshape(1, 1, 6 * D), mods_all[l, 1:1 + Bs].reshape(Bs, 1, 6 * D))
            for l in range(L)]
    hp = _modulate(xp, mods[0][0], 1, 0, Bp * Tp)
    hs = _modulate(xs, mods[0][1], 1, 0, Ts)

    new_k, new_v, new_ckv, new_kpe = [], [], [], []
    for l in range(L):
        w = _layer_weights(l, dims, w_in, w_s, b_s, w_uq, w_ukv, w_pa, w_pb, w_pc, w_o, w_gate, w_up, w_down)
        vecs = (sgu_ln_g[l], sgu_ln_b[l], q_norm_g[l], k_norm_g[l], mla_q_norm_g[l], mla_kv_norm_g[l],
                ln1_g[l], ln1_b[l], ln2_g[l], ln2_b[l])
        nxt_p = (mods[l + 1][0], 1, 0) if l + 1 < L else None
        nxt_s = (mods[l + 1][1], 1, 0) if l + 1 < L else None
        xp, hp, own = _trunk_layer(xp, hp, mods[l][0], w, vecs, dims, Bp, Tp, None, None, alpha, nxt_p)
        new_k.append(own[0].reshape(Bp, Tp, kvh, hd))
        new_v.append(own[1].reshape(Bp, Tp, kvh, hd))
        new_ckv.append(own[2].reshape(Bp, Tp, rank))
        new_kpe.append(own[3].reshape(Bp, Tp, rd))
        ctx = (cache_k[:, l], cache_v[:, l], cache_ckv[:, l], cache_kpe[:, l])
        xs, hs, _ = _trunk_layer(xs, hs, mods[l][1], w, vecs, dims, Bs, Ts, rope, ctx, alpha, nxt_s)

    return (xp.reshape(Bp, Tp, D), xs.reshape(Bs, Ts, D),
            jnp.stack(new_k, axis=1), jnp.stack(new_v, axis=1),
            jnp.stack(new_ckv, axis=1), jnp.stack(new_kpe, axis=1))
```

```python
import functools
from typing import NamedTuple

import jax
import jax.numpy as jnp
from jax import lax
from jax.experimental import pallas as pl
from jax.experimental.pallas import tpu as pltpu

F32 = jnp.float32
BF16 = jnp.bfloat16

NORM_EPS = 1e-6
ROPE_THETA = 10000.0
GRID_W = 64

LANES = 128
VMEM_LIMIT_BYTES = 56 * 1024 * 1024


def _cparams(n_axes):
    return pltpu.CompilerParams(
        dimension_semantics=("arbitrary",) * n_axes,
        vmem_limit_bytes=VMEM_LIMIT_BYTES,
    )


def _blk(n, pref):
    b = min(n, pref)
    while n % b:
        b //= 2
    return b


def _rms(a, g):
    ms = jnp.mean(a * a, axis=-1, keepdims=True)
    return a * lax.rsqrt(ms + NORM_EPS) * g


def _rope(y, cos, sin_signed, quarter):
    n = y.shape[-1]
    lane = lax.broadcasted_iota(jnp.int32, y.shape, 1)
    first = (lane & quarter) == 0
    rot = jnp.where(first, pltpu.roll(y, n - quarter, axis=1), pltpu.roll(y, quarter, axis=1))
    return y * cos + rot * sin_signed


def _sigmoid(x):
    return 1.0 / (1.0 + jnp.exp(-x))


def _layer_norm(z, g, b):
    mu = jnp.mean(z, axis=-1, keepdims=True)
    zc = z - mu
    var = jnp.mean(zc * zc, axis=-1, keepdims=True)
    return zc * lax.rsqrt(var + NORM_EPS) * g + b


def _ada_kernel(c_ref, w_ref, b_ref, o_ref):
    c = c_ref[...]
    s = (c * _sigmoid(c)).astype(BF16)
    o_ref[...] = jnp.dot(s, w_ref[...].astype(BF16), preferred_element_type=F32) + b_ref[...]


def _ada(cond, w_ada, b_ada):
    L, D, N = w_ada.shape
    R = cond.shape[0]
    bn = _blk(N, 512)
    return pl.pallas_call(
        _ada_kernel,
        grid=(L, N // bn),
        in_specs=[
            pl.BlockSpec((R, D), lambda l, j: (0, 0)),
            pl.BlockSpec((None, D, bn), lambda l, j: (l, 0, j)),
            pl.BlockSpec((None, 1, bn), lambda l, j: (l, 0, j)),
        ],
        out_specs=pl.BlockSpec((None, R, bn), lambda l, j: (l, 0, j)),
        out_shape=jax.ShapeDtypeStruct((L, R, N), F32),
        compiler_params=_cparams(2),
        name="ada_mod",
    )(cond, w_ada, b_ada.reshape(L, 1, N))


def _modulate_kernel(x_ref, sc_ref, sh_ref, h_ref):
    h_ref[...] = (x_ref[...] * (1.0 + sc_ref[...]) + sh_ref[...]).astype(h_ref.dtype)


def _mod_spec(D, k, rows_per_group, bt):
    return pl.BlockSpec((None, 1, D), lambda i, *_: ((i * bt) // rows_per_group, 0, k))


def _modulate(x, mods, k_sc, k_sh, rows_per_group):
    M, D = x.shape
    bt = _blk(rows_per_group, 512)
    return pl.pallas_call(
        _modulate_kernel,
        grid=(M // bt,),
        in_specs=[
            pl.BlockSpec((bt, D), lambda i: (i, 0)),
            _mod_spec(D, k_sc, rows_per_group, bt),
            _mod_spec(D, k_sh, rows_per_group, bt),
        ],
        out_specs=pl.BlockSpec((bt, D), lambda i: (i, 0)),
        out_shape=jax.ShapeDtypeStruct((M, D), BF16),
        compiler_params=_cparams(1),
        name="modulate",
    )(x, mods, mods)


def _ln_kernel(x_ref, y_ref, gate_ref, g_ref, b_ref, *rest, alpha, with_h):
    z = alpha * x_ref[...] + gate_ref[...] * y_ref[...]
    xn = _layer_norm(z, g_ref[...], b_ref[...])
    if with_h:
        sc_ref, sh_ref, xo_ref, h_ref = rest
        xo_ref[...] = xn
        h_ref[...] = (xn * (1.0 + sc_ref[...]) + sh_ref[...]).astype(h_ref.dtype)
    else:
        (xo_ref,) = rest
        xo_ref[...] = xn


def _ln_residual(x, y, mods, k_gate, ln_g, ln_b, alpha, rows_per_group, nxt=None):
    M, D = x.shape
    bt = _blk(rows_per_group, 256)
    row = pl.BlockSpec((bt, D), lambda i: (i, 0))
    vec = pl.BlockSpec((1, D), lambda i: (0, 0))
    in_specs = [row, row, _mod_spec(D, k_gate, rows_per_group, bt), vec, vec]
    args = [x, y, mods, ln_g.reshape(1, D), ln_b.reshape(1, D)]
    out_shape = [jax.ShapeDtypeStruct((M, D), F32)]
    out_specs = [row]
    if nxt is not None:
        mods_n, k_sc, k_sh = nxt
        in_specs += [_mod_spec(D, k_sc, rows_per_group, bt), _mod_spec(D, k_sh, rows_per_group, bt)]
        args += [mods_n, mods_n]
        out_shape.append(jax.ShapeDtypeStruct((M, D), BF16))
        out_specs.append(row)
    res = pl.pallas_call(
        functools.partial(_ln_kernel, alpha=alpha, with_h=nxt is not None),
        grid=(M // bt,),
        in_specs=in_specs,
        out_specs=out_specs,
        out_shape=out_shape,
        compiler_params=_cparams(1),
        name="ln_residual",
    )(*args)
    return res if nxt is not None else (res[0], None)


class _W(NamedTuple):
    arr: jax.Array
    layer: int
    col0: int
    n_cols: int


def _wfull(a):
    return _W(a[None], 0, 0, a.shape[1])


def _wspec(w, bn):
    K = w.arr.shape[1]
    l, ob = w.layer, w.col0 // bn
    assert w.col0 % bn == 0 and w.n_cols % bn == 0
    return pl.BlockSpec((None, K, bn), lambda i, j: (l, 0, ob + j))


def _mm_call(kernel, x, ws, bm, bn, extras, extra_specs, out_dtypes, name):
    M, K = x.shape
    n_cols = ws[0].n_cols
    in_specs = [pl.BlockSpec((bm, K), lambda i, j: (i, 0))]
    in_specs += [_wspec(w, bn) for w in ws]
    in_specs += list(extra_specs)
    out_shape = [jax.ShapeDtypeStruct((M, n_cols), dt) for dt in out_dtypes]
    out_specs = [pl.BlockSpec((bm, bn), lambda i, j: (i, j)) for _ in out_dtypes]
    return pl.pallas_call(
        kernel,
        grid=(M // bm, n_cols // bn),
        in_specs=in_specs,
        out_specs=out_specs,
        out_shape=out_shape,
        compiler_params=_cparams(2),
        name=name,
    )(x, *[w.arr for w in ws], *extras)


def _xw(x_ref, w_ref):
    return jnp.dot(x_ref[...], w_ref[...].astype(BF16), preferred_element_type=F32)


def _mm_plain_kernel(x_ref, w_ref, o_ref):
    o_ref[...] = _xw(x_ref, w_ref).astype(o_ref.dtype)


def _mm_plain(x, w, out_dtype, name, bm=1024, bn=1024):
    bm, bn = _blk(x.shape[0], bm), _blk(w.n_cols, bn)
    return _mm_call(_mm_plain_kernel, x, [w], bm, bn, [], [], [out_dtype], name)[0]


def _rope_specs(rope, bm, rows_per_batch):
    if rope is None:
        return [], []
    nb = rows_per_batch // bm
    spec = pl.BlockSpec((bm, LANES), lambda i, j: (i % nb, 0))
    return [rope[0], rope[1]], [spec, spec]


def _mm_heads_kernel(x_ref, w_ref, g_ref, *rest, n_norm, rope, quarter):
    if rope:
        cos_ref, sin_ref, *outs = rest
    else:
        outs = rest
    acc = _xw(x_ref, w_ref)
    for h in range(acc.shape[1] // LANES):
        a = acc[:, h * LANES:(h + 1) * LANES]
        if h < n_norm:
            a = _rms(a, g_ref[...])
            if rope:
                a = _rope(a, cos_ref[...], sin_ref[...], quarter)
        for o in outs:
            o[:, h * LANES:(h + 1) * LANES] = a.astype(o.dtype)


def _mm_heads(x, w, gain, rope, rows_per_batch, n_norm, out_dtypes, name, bm, bn):
    bm, bn = _blk(x.shape[0] if rope is None else rows_per_batch, bm), _blk(w.n_cols, bn)
    rargs, rspecs = _rope_specs(rope, bm, rows_per_batch)
    kern = functools.partial(_mm_heads_kernel, n_norm=n_norm, rope=rope is not None,
                             quarter=LANES // 4)
    return _mm_call(kern, x, [w], bm, bn,
                    [gain.reshape(1, LANES)] + rargs,
                    [pl.BlockSpec((1, LANES), lambda i, j: (0, 0))] + rspecs,
                    out_dtypes, name)


def _mm_rms_kernel(x_ref, w_ref, g_ref, o_ref):
    o_ref[...] = _rms(_xw(x_ref, w_ref), g_ref[...]).astype(o_ref.dtype)


def _mm_rms(x, w, gain, out_dtype, name, bm=512):
    N = w.n_cols
    bm = _blk(x.shape[0], bm)
    return _mm_call(_mm_rms_kernel, x, [w], bm, N, [gain.reshape(1, N)],
                    [pl.BlockSpec((1, N), lambda i, j: (0, 0))], [out_dtype], name)[0]


def _mm_ckv_kernel(x_ref, w_ref, g_ref, *rest, rank, rope_dim, rope):
    if rope:
        cos_ref, sin_ref, ckv_f, ckv_b, kpe_f, kpe_b = rest
    else:
        ckv_f, ckv_b, kpe_f, kpe_b = rest
    acc = jnp.dot(x_ref[...], w_ref[:, :rank + LANES].astype(BF16), preferred_element_type=F32)
    ckv = _rms(acc[:, :rank], g_ref[...])
    kpe = acc[:, rank:rank + LANES]
    lane = lax.broadcasted_iota(jnp.int32, kpe.shape, 1)
    kpe = jnp.where(lane < rope_dim, kpe, 0.0)
    if rope:
        kpe = _rope(kpe, cos_ref[...], sin_ref[...], rope_dim // 4)
    ckv_f[...] = ckv
    ckv_b[...] = ckv.astype(BF16)
    kpe_f[...] = kpe
    kpe_b[...] = kpe.astype(BF16)


def _mm_ckv(x, w, gain, rope, rows_per_batch, rank, rope_dim, name, bm=512):
    M, K = x.shape
    bw = w.n_cols
    assert bw >= rank + LANES and w.col0 % bw == 0
    bm = _blk(M if rope is None else rows_per_batch, bm)
    rargs, rspecs = _rope_specs(rope, bm, rows_per_batch)
    kern = functools.partial(_mm_ckv_kernel, rank=rank, rope_dim=rope_dim, rope=rope is not None)
    in_specs = [pl.BlockSpec((bm, K), lambda i, j: (i, 0)),
                _wspec(w, bw),
                pl.BlockSpec((1, rank), lambda i, j: (0, 0))] + rspecs
    out_shape = [jax.ShapeDtypeStruct((M, rank), F32), jax.ShapeDtypeStruct((M, rank), BF16),
                 jax.ShapeDtypeStruct((M, LANES), F32), jax.ShapeDtypeStruct((M, LANES), BF16)]
    out_specs = [pl.BlockSpec((bm, rank), lambda i, j: (i, 0)), pl.BlockSpec((bm, rank), lambda i, j: (i, 0)),
                 pl.BlockSpec((bm, LANES), lambda i, j: (i, 0)), pl.BlockSpec((bm, LANES), lambda i, j: (i, 0))]
    return pl.pallas_call(
        kern, grid=(M // bm, 1), in_specs=in_specs, out_specs=out_specs, out_shape=out_shape,
        compiler_params=_cparams(2), name=name,
    )(x, w.arr, gain.reshape(1, rank), *rargs)


def _mm_uq_kernel(x_ref, w_ref, *rest, rope, quarter):
    if rope:
        cos_ref, sin_ref, o_ref = rest
    else:
        (o_ref,) = rest
    acc = _xw(x_ref, w_ref)
    for h in range(acc.shape[1] // LANES):
        a = acc[:, h * LANES:(h + 1) * LANES]
        if rope and h % 2 == 1:
            a = _rope(a, cos_ref[...], sin_ref[...], quarter)
        o_ref[:, h * LANES:(h + 1) * LANES] = a.astype(o_ref.dtype)


def _mm_uq(x, w, rope, rows_per_batch, rope_dim, name, bm=1024, bn=1024):
    bm, bn = _blk(x.shape[0] if rope is None else rows_per_batch, bm), _blk(w.n_cols, bn)
    rargs, rspecs = _rope_specs(rope, bm, rows_per_batch)
    kern = functools.partial(_mm_uq_kernel, rope=rope is not None, quarter=rope_dim // 4)
    return _mm_call(kern, x, [w], bm, bn, rargs, rspecs, [BF16], name)[0]


def _mm_sigmoid_kernel(x_ref, w_ref, o_ref):
    o_ref[...] = _sigmoid(_xw(x_ref, w_ref)).astype(o_ref.dtype)


def _mm_sigmoid(x, w, out_dtype, name, bm=1024, bn=1024):
    bm, bn = _blk(x.shape[0], bm), _blk(w.n_cols, bn)
    return _mm_call(_mm_sigmoid_kernel, x, [w], bm, bn, [], [], [out_dtype], name)[0]


def _mm_swiglu_kernel(x_ref, wg_ref, wu_ref, o_ref):
    a = _xw(x_ref, wg_ref)
    b = _xw(x_ref, wu_ref)
    o_ref[...] = (a * _sigmoid(a) * b).astype(o_ref.dtype)


def _mm_swiglu(x, wg, wu, name, bm=1024, bn=256):
    bm, bn = _blk(x.shape[0], bm), _blk(wg.n_cols, bn)
    return _mm_call(_mm_swiglu_kernel, x, [wg, wu], bm, bn, [], [], [BF16], name)[0]


def _mm_merge_kernel(ya_ref, yb_ref, yc_ref, wa_ref, wb_ref, wc_ref, ga_ref, gb_ref, gc_ref, o_ref):
    a = jnp.dot(ya_ref[...], wa_ref[...], preferred_element_type=F32)
    b = jnp.dot(yb_ref[...], wb_ref[...], preferred_element_type=F32)
    c = jnp.dot(yc_ref[...], wc_ref[...], preferred_element_type=F32)
    o_ref[...] = (ga_ref[...] * a + gb_ref[...] * b + gc_ref[...] * c).astype(o_ref.dtype)


def _mm_merge(ya, yb, yc, wa, wb, wc, gates, name, bm=512, bn=512):
    M = ya.shape[0]
    D = wa.shape[1]
    bm, bn = _blk(M, bm), _blk(D, bn)
    nb = D // bn
    xs = lambda y: pl.BlockSpec((bm, y.shape[1]), lambda i, j: (i, 0))
    wsp = lambda w: pl.BlockSpec((w.shape[0], bn), lambda i, j: (0, j))
    gsp = lambda t: pl.BlockSpec((bm, bn), lambda i, j: (i, j + t * nb))
    return pl.pallas_call(
        _mm_merge_kernel,
        grid=(M // bm, nb),
        in_specs=[xs(ya), xs(yb), xs(yc), wsp(wa), wsp(wb), wsp(wc), gsp(0), gsp(1), gsp(2)],
        out_specs=pl.BlockSpec((bm, bn), lambda i, j: (i, j)),
        out_shape=jax.ShapeDtypeStruct((M, D), BF16),
        compiler_params=_cparams(2),
        name=name,
    )(ya, yb, yc, wa, wb, wc, gates, gates, gates)


def _sgu_kernel(u_ref, v_ref, lg_ref, lb_ref, ws_ref, bias_ref, o_ref, *, chunk, groups):
    vn = _layer_norm(v_ref[...], lg_ref[...], lb_ref[...]).astype(BF16)
    gd = vn.shape[1] // groups
    for c in range(vn.shape[0] // chunk):
        r0 = c * chunk
        for g in range(groups):
            c0 = g * gd
            s = jnp.dot(ws_ref[g], vn[r0:r0 + chunk, c0:c0 + gd], preferred_element_type=F32)
            s = s + bias_ref[:, c0:c0 + gd]
            o_ref[r0:r0 + chunk, c0:c0 + gd] = (u_ref[r0:r0 + chunk, c0:c0 + gd] * s).astype(o_ref.dtype)


def _sgu(uv, ln_g, ln_b, w_s, bias_full, name):
    M = uv.shape[0]
    A = uv.shape[1] // 2
    G, C, _ = w_s.shape
    bt = _blk(M, 2 * C)
    return pl.pallas_call(
        functools.partial(_sgu_kernel, chunk=C, groups=G),
        grid=(M // bt,),
        in_specs=[
            pl.BlockSpec((bt, A), lambda i: (i, 0)),
            pl.BlockSpec((bt, A), lambda i: (i, 1)),
            pl.BlockSpec((1, A), lambda i: (0, 0)),
            pl.BlockSpec((1, A), lambda i: (0, 0)),
            pl.BlockSpec((G, C, C), lambda i: (0, 0, 0)),
            pl.BlockSpec((C, A), lambda i: (0, 0)),
        ],
        out_specs=pl.BlockSpec((bt, A), lambda i: (i, 0)),
        out_shape=jax.ShapeDtypeStruct((M, A), BF16),
        compiler_params=_cparams(1),
        name=name,
    )(uv, uv, ln_g.reshape(1, A), ln_b.reshape(1, A), w_s, bias_full)


_NT = (((1,), (1,)), ((), ()))
_LOG2E = 1.4426950408889634


def _attend(q, k, v, scale):
    s = lax.dot_general(q, k, _NT, preferred_element_type=F32)
    m = jnp.max(s, axis=-1, keepdims=True)
    p = jnp.exp2((s - m) * (scale * _LOG2E))
    l = jnp.sum(p, axis=-1, keepdims=True)
    return jnp.dot(p.astype(BF16), v, preferred_element_type=F32) / l


def _chunks(n_heads, rows, sub):
    out = [(h, r, sub) for h in range(n_heads) for r in range(0, rows, sub)]
    if sub >= 512 and len(out) >= 4:
        h, r, _ = out[0]
        out[0:1] = [(h, r, sub // 4), (h, r + sub // 4, sub - sub // 4)]
        h, r, _ = out[-1]
        out[-1:] = [(h, r, sub - sub // 4), (h, r + sub - sub // 4, sub // 4)]
    return out


def _gqa_kernel(q_ref, k_ref, v_ref, o_ref, *, nkv, group, hd, scale, sub):
    for h, r, n in _chunks(nkv * group, q_ref.shape[0], sub):
        kv = h // group
        o = _attend(q_ref[r:r + n, h * hd:(h + 1) * hd], k_ref[:, kv * hd:(kv + 1) * hd],
                    v_ref[:, kv * hd:(kv + 1) * hd], scale)
        o_ref[r:r + n, h * hd:(h + 1) * hd] = o.astype(o_ref.dtype)


def _gqa(q, k, v, kv_heads, hd, name, bq, nkv, sub=512):
    B, Nq, W = q.shape
    S = k.shape[1]
    group = W // (kv_heads * hd)
    bq = _blk(Nq, bq)
    gw = nkv * group * hd
    sub = _blk(bq, sub)
    return pl.pallas_call(
        functools.partial(_gqa_kernel, nkv=nkv, group=group, hd=hd, scale=hd ** -0.5, sub=sub),
        grid=(B, kv_heads // nkv, Nq // bq),
        in_specs=[
            pl.BlockSpec((None, bq, gw), lambda b, n, i: (b, i, n)),
            pl.BlockSpec((None, S, nkv * hd), lambda b, n, i: (b, 0, n)),
            pl.BlockSpec((None, S, nkv * hd), lambda b, n, i: (b, 0, n)),
        ],
        out_specs=pl.BlockSpec((None, bq, gw), lambda b, n, i: (b, i, n)),
        out_shape=jax.ShapeDtypeStruct((B, Nq, W), BF16),
        compiler_params=_cparams(3),
        name=name,
    )(q, k, v)


def _mla_kernel(q_ref, kn_ref, kpe_ref, v_ref, o_ref, *, hb, scale, sub):
    kpe = kpe_ref[...]
    ks = [jnp.concatenate([kn_ref[:, h * LANES:(h + 1) * LANES], kpe], axis=1) for h in range(hb)]
    for h, r, n in _chunks(hb, q_ref.shape[0], sub):
        o = _attend(q_ref[r:r + n, 2 * h * LANES:2 * (h + 1) * LANES], ks[h],
                    v_ref[:, h * LANES:(h + 1) * LANES], scale)
        o_ref[r:r + n, h * LANES:(h + 1) * LANES] = o.astype(o_ref.dtype)


def _mla(q, kv, kpe, heads, scale, name, bq, hb, sub=512):
    B, Nq, _ = q.shape
    S = kv.shape[1]
    bq = _blk(Nq, bq)
    sub = _blk(bq, sub)
    nhb = heads // hb
    return pl.pallas_call(
        functools.partial(_mla_kernel, hb=hb, scale=scale, sub=sub),
        grid=(B, nhb, Nq // bq),
        in_specs=[
            pl.BlockSpec((None, bq, 2 * hb * LANES), lambda b, h, i: (b, i, h)),
            pl.BlockSpec((None, S, hb * LANES), lambda b, h, i: (b, 0, h)),
            pl.BlockSpec((None, S, LANES), lambda b, h, i: (b, 0, 0)),
            pl.BlockSpec((None, S, hb * LANES), lambda b, h, i: (b, 0, nhb + h)),
        ],
        out_specs=pl.BlockSpec((None, bq, hb * LANES), lambda b, h, i: (b, i, h)),
        out_shape=jax.ShapeDtypeStruct((B, Nq, heads * LANES), BF16),
        compiler_params=_cparams(3),
        name=name,
    )(q, kv, kpe, kv)


def _rope_tables(n, dim):
    rows = n // GRID_W
    row = jnp.repeat(jnp.arange(rows, dtype=F32), GRID_W)
    col = jnp.tile(jnp.arange(GRID_W, dtype=F32), rows)
    quarter = dim // 4
    freqs = ROPE_THETA ** (-jnp.arange(quarter, dtype=F32) / quarter)
    ra = row[:, None] * freqs[None, :]
    ca = col[:, None] * freqs[None, :]
    ang = jnp.concatenate([ra, ra, ca, ca], axis=-1)
    sign = jnp.where((jnp.arange(dim) // quarter) % 2 == 0, -1.0, 1.0).astype(F32)
    cos = jnp.cos(ang)
    sin = jnp.sin(ang) * sign[None, :]
    if dim < LANES:
        cos = jnp.pad(cos, ((0, 0), (0, LANES - dim)), constant_values=1.0)
        sin = jnp.pad(sin, ((0, 0), (0, LANES - dim)))
    return cos, sin


def _layer_weights(l, dims, w_in, w_s, b_s, w_uq, w_ukv, w_pa, w_pb, w_pc, w_o, w_gate, w_up, w_down):
    A, qw, kvw, qr, rank, rd, D, mh, nope, vd = dims
    o = [0, 2 * A, 2 * A + qw, 2 * A + qw + 2 * kvw, 2 * A + qw + 2 * kvw + qr]
    o.append(o[-1] + rank + rd)
    bf = lambda a: a.astype(BF16)
    w = {}
    w['uv'] = _W(w_in, l, o[0], 2 * A)
    w['q'] = _W(w_in, l, o[1], qw)
    w['kv'] = _W(w_in, l, o[2], 2 * kvw)
    w['cq'] = _W(w_in, l, o[3], qr)
    w['ckv'] = _W(w_in, l, o[4], 2 * rank)
    w['g'] = _wfull(bf(w_in[l][:, o[5]:]))
    w['o'] = _W(w_o, l, 0, w_o.shape[2])
    w['gate'] = _W(w_gate, l, 0, w_gate.shape[2])
    w['up'] = _W(w_up, l, 0, w_up.shape[2])
    uq = w_uq[l].reshape(qr, mh, nope + rd)
    w['uq'] = _wfull(bf(jnp.pad(uq, ((0, 0), (0, 0), (0, 2 * LANES - nope - rd))).reshape(qr, mh * 2 * LANES)))
    ukv = w_ukv[l].reshape(rank, mh, nope + vd)
    w['ukv'] = _wfull(bf(jnp.concatenate([ukv[:, :, :nope].reshape(rank, mh * nope),
                                          ukv[:, :, nope:].reshape(rank, mh * vd)], axis=1)))
    w['pa'], w['pb'], w['pc'] = bf(w_pa[l]), bf(w_pb[l]), bf(w_pc[l])
    w['down'] = _wfull(bf(w_down[l]))
    w['s'] = bf(w_s[l])
    gd = A // w_s.shape[1]
    w['sb'] = jnp.repeat(b_s[l].T, gd, axis=1)
    return w


def _trunk_layer(x, h, mods, w, vecs, dims, B, T, rope, ctx, alpha, nxt):
    A, qw, kvw, qr, rank, rd, D, mh, nope, vd = dims
    sgu_g, sgu_b, qg, kg, cqg, ckvg, ln1g, ln1b, ln2g, ln2b = vecs
    M = B * T
    G = mods.shape[0]
    rpg = M // G
    hd = LANES
    kvh = kvw // hd
    tag = "s" if rope is not None else "p"
    rope_g = None if rope is None else rope[0]
    rope_m = None if rope is None else rope[1]

    uv = _mm_plain(h, w['uv'], F32, "proj_uv_" + tag, bn=512)
    q = _mm_heads(h, w['q'], qg, rope_g, T, 1 << 30, [BF16], "proj_q_" + tag, bm=1024, bn=512)[0]
    kv_f, kv_b = _mm_heads(h, w['kv'], kg, rope_g, T, kvh, [F32, BF16], "proj_kv_" + tag,
                           bm=512, bn=2 * kvw)
    cq = _mm_rms(h, w['cq'], cqg, BF16, "proj_cq_" + tag)
    ckv_f, ckv_b, kpe_f, kpe_b = _mm_ckv(h, w['ckv'], ckvg, rope_m, T, rank, rd, "proj_ckv_" + tag)
    gates = _mm_sigmoid(h, w['g'], F32, "proj_gates_" + tag)

    y_a = _sgu(uv, sgu_g, sgu_b, w['s'], w['sb'], "sgu_" + tag)

    k_b = kv_b[:, :kvw].reshape(B, T, kvw)
    v_b = kv_b[:, kvw:].reshape(B, T, kvw)
    if ctx is not None:
        c_k, c_v, c_ckv, c_kpe = ctx
        P = c_k.shape[1]
        k_b = jnp.concatenate([c_k.reshape(B, P, kvw).astype(BF16), k_b], axis=1)
        v_b = jnp.concatenate([c_v.reshape(B, P, kvw).astype(BF16), v_b], axis=1)
    long_keys = ctx is not None
    y_b = _gqa(q.reshape(B, T, qw), k_b, v_b, kvh, hd, "gqa_" + tag,
               bq=1024 if long_keys else 256, nkv=1 if long_keys else kvh).reshape(M, qw)

    qc = _mm_uq(cq, w['uq'], rope_m, T, rd, "mla_uq_" + tag)
    ckv_all = ckv_b.reshape(B, T, rank)
    kpe_all = kpe_b.reshape(B, T, LANES)
    if ctx is not None:
        ckv_all = jnp.concatenate([c_ckv.astype(BF16), ckv_all], axis=1)
        c_kpe_pad = jnp.pad(c_kpe, ((0, 0), (0, 0), (0, LANES - rd))).astype(BF16)
        kpe_all = jnp.concatenate([c_kpe_pad, kpe_all], axis=1)
    S = ckv_all.shape[1]
    kvu = _mm_plain(ckv_all.reshape(B * S, rank), w['ukv'], BF16, "mla_ukv_" + tag)
    y_c = _mla(qc.reshape(B, T, mh * 2 * LANES), kvu.reshape(B, S, mh * (nope + vd)), kpe_all,
               mh, (nope + rd) ** -0.5, "mla_" + tag,
               bq=4096 if long_keys else 256, hb=1 if long_keys else mh).reshape(M, mh * vd)

    merged = _mm_merge(y_a, y_b, y_c, w['pa'], w['pb'], w['pc'], gates, "merge_" + tag)
    mix = _mm_plain(merged, w['o'], F32, "proj_o_" + tag, bn=512)
    x1, h2 = _ln_residual(x, mix, mods, 2, ln1g, ln1b, alpha, rpg, nxt=(mods, 4, 3))
    hid = _mm_swiglu(h2, w['gate'], w['up'], "ffn_in_" + tag)
    ff = _mm_plain(hid, w['down'], F32, "ffn_out_" + tag, bm=512, bn=512)
    x2, h_next = _ln_residual(x1, ff, mods, 5, ln2g, ln2b, alpha, rpg, nxt=nxt)

    own = (kv_f[:, :kvw], kv_f[:, kvw:], ckv_f, kpe_f[:, :rd])
    return x2, h_next, own


def kernel(x_prompt, x_sample, cache_k, cache_v, cache_ckv, cache_kpe, c, c_ctx,
           w_ada, b_ada, w_in, sgu_ln_g, sgu_ln_b, w_s, b_s, q_norm_g, k_norm_g,
           mla_q_norm_g, mla_kv_norm_g, w_uq, w_ukv, w_pa, w_pb, w_pc, w_o,
           ln1_g, ln1_b, ln2_g, ln2_b, w_gate, w_up, w_down):
    Bp, Tp, D = x_prompt.shape
    Bs, Ts, _ = x_sample.shape
    L = w_ada.shape[0]
    A = sgu_ln_g.shape[1]
    hd = q_norm_g.shape[1]
    kvh = cache_k.shape[3]
    qw = w_pb.shape[1]
    kvw = kvh * hd
    qr = mla_q_norm_g.shape[1]
    rank = mla_kv_norm_g.shape[1]
    rd = cache_kpe.shape[-1]
    vd = LANES
    mh = w_pc.shape[1] // vd
    nope = w_uq.shape[2] // mh - rd
    assert hd == LANES and nope == LANES and w_ukv.shape[2] == mh * (nope + vd)
    dims = (A, qw, kvw, qr, rank, rd, D, mh, nope, vd)
    alpha = float((2 * L) ** 0.25)

    R = -(-(1 + Bs) // 8) * 8
    cond = jnp.concatenate([c_ctx[None, :], c, jnp.zeros((R - 1 - Bs, D), F32)], axis=0)
    mods_all = _ada(cond, w_ada, b_ada)

    rope = (_rope_tables(Ts, hd), _rope_tables(Ts, rd))

    xp = x_prompt.reshape(Bp * Tp, D)
    xs = x_sample.reshape(Bs * Ts, D)
    mods = [(mods_all[l, 0:1].reshape(1, 1, 6 * D), mods_all[l, 1:1 + Bs].reshape(Bs, 1, 6 * D))
            for l in range(L)]
    hp = _modulate(xp, mods[0][0], 1, 0, Bp * Tp)
    hs = _modulate(xs, mods[0][1], 1, 0, Ts)

    new_k, new_v, new_ckv, new_kpe = [], [], [], []
    for l in range(L):
        w = _layer_weights(l, dims, w_in, w_s, b_s, w_uq, w_ukv, w_pa, w_pb, w_pc, w_o, w_gate, w_up, w_down)
        vecs = (sgu_ln_g[l], sgu_ln_b[l], q_norm_g[l], k_norm_g[l], mla_q_norm_g[l], mla_kv_norm_g[l],
                ln1_g[l], ln1_b[l], ln2_g[l], ln2_b[l])
        nxt_p = (mods[l + 1][0], 1, 0) if l + 1 < L else None
        nxt_s = (mods[l + 1][1], 1, 0) if l + 1 < L else None
        xp, hp, own = _trunk_layer(xp, hp, mods[l][0], w, vecs, dims, Bp, Tp, None, None, alpha, nxt_p)
        new_k.append(own[0].reshape(Bp, Tp, kvh, hd))
        new_v.append(own[1].reshape(Bp, Tp, kvh, hd))
        new_ckv.append(own[2].reshape(Bp, Tp, rank))
        new_kpe.append(own[3].reshape(Bp, Tp, rd))
        ctx = (cache_k[:, l], cache_v[:, l], cache_ckv[:, l], cache_kpe[:, l])
        xs, hs, _ = _trunk_layer(xs, hs, mods[l][1], w, vecs, dims, Bs, Ts, rope, ctx, alpha, nxt_s)

    return (xp.reshape(Bp, Tp, D), xs.reshape(Bs, Ts, D),
            jnp.stack(new_k, axis=1), jnp.stack(new_v, axis=1),
            jnp.stack(new_ckv, axis=1), jnp.stack(new_kpe, axis=1))
```

```python
import functools
from typing import NamedTuple

import jax
import jax.numpy as jnp
from jax import lax
from jax.experimental import pallas as pl
from jax.experimental.pallas import tpu as pltpu

F32 = jnp.float32
BF16 = jnp.bfloat16

NORM_EPS = 1e-6
ROPE_THETA = 10000.0
GRID_W = 64

LANES = 128
VMEM_LIMIT_BYTES = 56 * 1024 * 1024


def _cparams(n_axes):
    return pltpu.CompilerParams(
        dimension_semantics=("arbitrary",) * n_axes,
        vmem_limit_bytes=VMEM_LIMIT_BYTES,
    )


def _blk(n, pref):
    b = min(n, pref)
    while n % b:
        b //= 2
    return b


def _rms(a, g):
    ms = jnp.mean(a * a, axis=-1, keepdims=True)
    return a * lax.rsqrt(ms + NORM_EPS) * g


def _rope(y, cos, sin_signed, quarter):
    n = y.shape[-1]
    lane = lax.broadcasted_iota(jnp.int32, y.shape, 1)
    first = (lane & quarter) == 0
    rot = jnp.where(first, pltpu.roll(y, n - quarter, axis=1), pltpu.roll(y, quarter, axis=1))
    return y * cos + rot * sin_signed


def _sigmoid(x):
    return 1.0 / (1.0 + jnp.exp(-x))


def _layer_norm(z, g, b):
    mu = jnp.mean(z, axis=-1, keepdims=True)
    zc = z - mu
    var = jnp.mean(zc * zc, axis=-1, keepdims=True)
    return zc * lax.rsqrt(var + NORM_EPS) * g + b


def _ada_kernel(c_ref, w_ref, b_ref, o_ref):
    c = c_ref[...]
    s = (c * _sigmoid(c)).astype(BF16)
    o_ref[...] = jnp.dot(s, w_ref[...].astype(BF16), preferred_element_type=F32) + b_ref[...]


def _ada(cond, w_ada, b_ada):
    L, D, N = w_ada.shape
    R = cond.shape[0]
    bn = _blk(N, 512)
    return pl.pallas_call(
        _ada_kernel,
        grid=(L, N // bn),
        in_specs=[
            pl.BlockSpec((R, D), lambda l, j: (0, 0)),
            pl.BlockSpec((None, D, bn), lambda l, j: (l, 0, j)),
            pl.BlockSpec((None, 1, bn), lambda l, j: (l, 0, j)),
        ],
        out_specs=pl.BlockSpec((None, R, bn), lambda l, j: (l, 0, j)),
        out_shape=jax.ShapeDtypeStruct((L, R, N), F32),
        compiler_params=_cparams(2),
        name="ada_mod",
    )(cond, w_ada, b_ada.reshape(L, 1, N))


def _modulate_kernel(x_ref, sc_ref, sh_ref, h_ref):
    h_ref[...] = (x_ref[...] * (1.0 + sc_ref[...]) + sh_ref[...]).astype(h_ref.dtype)


def _mod_spec(D, k, rows_per_group, bt):
    return pl.BlockSpec((None, 1, D), lambda i, *_: ((i * bt) // rows_per_group, 0, k))


def _modulate(x, mods, k_sc, k_sh, rows_per_group):
    M, D = x.shape
    bt = _blk(rows_per_group, 512)
    return pl.pallas_call(
        _modulate_kernel,
        grid=(M // bt,),
        in_specs=[
            pl.BlockSpec((bt, D), lambda i: (i, 0)),
            _mod_spec(D, k_sc, rows_per_group, bt),
            _mod_spec(D, k_sh, rows_per_group, bt),
        ],
        out_specs=pl.BlockSpec((bt, D), lambda i: (i, 0)),
        out_shape=jax.ShapeDtypeStruct((M, D), BF16),
        compiler_params=_cparams(1),
        name="modulate",
    )(x, mods, mods)


def _ln_kernel(x_ref, y_ref, gate_ref, g_ref, b_ref, *rest, alpha, with_h):
    z = alpha * x_ref[...] + gate_ref[...] * y_ref[...]
    xn = _layer_norm(z, g_ref[...], b_ref[...])
    if with_h:
        sc_ref, sh_ref, xo_ref, h_ref = rest
        xo_ref[...] = xn
        h_ref[...] = (xn * (1.0 + sc_ref[...]) + sh_ref[...]).astype(h_ref.dtype)
    else:
        (xo_ref,) = rest
        xo_ref[...] = xn


def _ln_residual(x, y, mods, k_gate, ln_g, ln_b, alpha, rows_per_group, nxt=None):
    M, D = x.shape
    bt = _blk(rows_per_group, 256)
    row = pl.BlockSpec((bt, D), lambda i: (i, 0))
    vec = pl.BlockSpec((1, D), lambda i: (0, 0))
    in_specs = [row, row, _mod_spec(D, k_gate, rows_per_group, bt), vec, vec]
    args = [x, y, mods, ln_g.reshape(1, D), ln_b.reshape(1, D)]
    out_shape = [jax.ShapeDtypeStruct((M, D), F32)]
    out_specs = [row]
    if nxt is not None:
        mods_n, k_sc, k_sh = nxt
        in_specs += [_mod_spec(D, k_sc, rows_per_group, bt), _mod_spec(D, k_sh, rows_per_group, bt)]
        args += [mods_n, mods_n]
        out_shape.append(jax.ShapeDtypeStruct((M, D), BF16))
        out_specs.append(row)
    res = pl.pallas_call(
        functools.partial(_ln_kernel, alpha=alpha, with_h=nxt is not None),
        grid=(M // bt,),
        in_specs=in_specs,
        out_specs=out_specs,
        out_shape=out_shape,
        compiler_params=_cparams(1),
        name="ln_residual",
    )(*args)
    return res if nxt is not None else (res[0], None)


class _W(NamedTuple):
    arr: jax.Array
    layer: int
    col0: int
    n_cols: int


def _wfull(a):
    return _W(a[None], 0, 0, a.shape[1])


def _wspec(w, bn):
    K = w.arr.shape[1]
    l, ob = w.layer, w.col0 // bn
    assert w.col0 % bn == 0 and w.n_cols % bn == 0
    return pl.BlockSpec((None, K, bn), lambda i, j: (l, 0, ob + j))


def _mm_call(kernel, x, ws, bm, bn, extras, extra_specs, out_dtypes, name):
    M, K = x.shape
    n_cols = ws[0].n_cols
    in_specs = [pl.BlockSpec((bm, K), lambda i, j: (i, 0))]
    in_specs += [_wspec(w, bn) for w in ws]
    in_specs += list(extra_specs)
    out_shape = [jax.ShapeDtypeStruct((M, n_cols), dt) for dt in out_dtypes]
    out_specs = [pl.BlockSpec((bm, bn), lambda i, j: (i, j)) for _ in out_dtypes]
    return pl.pallas_call(
        kernel,
        grid=(M // bm, n_cols // bn),
        in_specs=in_specs,
        out_specs=out_specs,
        out_shape=out_shape,
        compiler_params=_cparams(2),
        name=name,
    )(x, *[w.arr for w in ws], *extras)


def _xw(x_ref, w_ref):
    return jnp.dot(x_ref[...], w_ref[...].astype(BF16), preferred_element_type=F32)


def _mm_plain_kernel(x_ref, w_ref, o_ref):
    o_ref[...] = _xw(x_ref, w_ref).astype(o_ref.dtype)


def _mm_plain(x, w, out_dtype, name, bm=1024, bn=1024):
    bm, bn = _blk(x.shape[0], bm), _blk(w.n_cols, bn)
    return _mm_call(_mm_plain_kernel, x, [w], bm, bn, [], [], [out_dtype], name)[0]


def _rope_specs(rope, bm, rows_per_batch):
    if rope is None:
        return [], []
    nb = rows_per_batch // bm
    spec = pl.BlockSpec((bm, LANES), lambda i, j: (i % nb, 0))
    return [rope[0], rope[1]], [spec, spec]


def _mm_heads_kernel(x_ref, w_ref, g_ref, *rest, n_norm, rope, quarter):
    if rope:
        cos_ref, sin_ref, *outs = rest
    else:
        outs = rest
    acc = _xw(x_ref, w_ref)
    for h in range(acc.shape[1] // LANES):
        a = acc[:, h * LANES:(h + 1) * LANES]
        if h < n_norm:
            a = _rms(a, g_ref[...])
            if rope:
                a = _rope(a, cos_ref[...], sin_ref[...], quarter)
        for o in outs:
            o[:, h * LANES:(h + 1) * LANES] = a.astype(o.dtype)


def _mm_heads(x, w, gain, rope, rows_per_batch, n_norm, out_dtypes, name, bm, bn):
    bm, bn = _blk(x.shape[0] if rope is None else rows_per_batch, bm), _blk(w.n_cols, bn)
    rargs, rspecs = _rope_specs(rope, bm, rows_per_batch)
    kern = functools.partial(_mm_heads_kernel, n_norm=n_norm, rope=rope is not None,
                             quarter=LANES // 4)
    return _mm_call(kern, x, [w], bm, bn,
                    [gain.reshape(1, LANES)] + rargs,
                    [pl.BlockSpec((1, LANES), lambda i, j: (0, 0))] + rspecs,
                    out_dtypes, name)


def _mm_rms_kernel(x_ref, w_ref, g_ref, o_ref):
    o_ref[...] = _rms(_xw(x_ref, w_ref), g_ref[...]).astype(o_ref.dtype)


def _mm_rms(x, w, gain, out_dtype, name, bm=1024):
    N = w.n_cols
    bm = _blk(x.shape[0], bm)
    return _mm_call(_mm_rms_kernel, x, [w], bm, N, [gain.reshape(1, N)],
                    [pl.BlockSpec((1, N), lambda i, j: (0, 0))], [out_dtype], name)[0]


def _mm_ckv_kernel(x_ref, w_ref, g_ref, *rest, rank, rope_dim, rope):
    if rope:
        cos_ref, sin_ref, ckv_f, ckv_b, kpe_f, kpe_b = rest
    else:
        ckv_f, ckv_b, kpe_f, kpe_b = rest
    acc = jnp.dot(x_ref[...], w_ref[:, :rank + LANES].astype(BF16), preferred_element_type=F32)
    ckv = _rms(acc[:, :rank], g_ref[...])
    kpe = acc[:, rank:rank + LANES]
    lane = lax.broadcasted_iota(jnp.int32, kpe.shape, 1)
    kpe = jnp.where(lane < rope_dim, kpe, 0.0)
    if rope:
        kpe = _rope(kpe, cos_ref[...], sin_ref[...], rope_dim // 4)
    ckv_f[...] = ckv
    ckv_b[...] = ckv.astype(BF16)
    kpe_f[...] = kpe
    kpe_b[...] = kpe.astype(BF16)


def _mm_ckv(x, w, gain, rope, rows_per_batch, rank, rope_dim, name, bm=1024):
    M, K = x.shape
    bw = w.n_cols
    assert bw >= rank + LANES and w.col0 % bw == 0
    bm = _blk(M if rope is None else rows_per_batch, bm)
    rargs, rspecs = _rope_specs(rope, bm, rows_per_batch)
    kern = functools.partial(_mm_ckv_kernel, rank=rank, rope_dim=rope_dim, rope=rope is not None)
    in_specs = [pl.BlockSpec((bm, K), lambda i, j: (i, 0)),
                _wspec(w, bw),
                pl.BlockSpec((1, rank), lambda i, j: (0, 0))] + rspecs
    out_shape = [jax.ShapeDtypeStruct((M, rank), F32), jax.ShapeDtypeStruct((M, rank), BF16),
                 jax.ShapeDtypeStruct((M, LANES), F32), jax.ShapeDtypeStruct((M, LANES), BF16)]
    out_specs = [pl.BlockSpec((bm, rank), lambda i, j: (i, 0)), pl.BlockSpec((bm, rank), lambda i, j: (i, 0)),
                 pl.BlockSpec((bm, LANES), lambda i, j: (i, 0)), pl.BlockSpec((bm, LANES), lambda i, j: (i, 0))]
    return pl.pallas_call(
        kern, grid=(M // bm, 1), in_specs=in_specs, out_specs=out_specs, out_shape=out_shape,
        compiler_params=_cparams(2), name=name,
    )(x, w.arr, gain.reshape(1, rank), *rargs)


def _mm_uq_kernel(x_ref, w_ref, *rest, rope, quarter):
    if rope:
        cos_ref, sin_ref, o_ref = rest
    else:
        (o_ref,) = rest
    acc = _xw(x_ref, w_ref)
    for h in range(acc.shape[1] // LANES):
        a = acc[:, h * LANES:(h + 1) * LANES]
        if rope and h % 2 == 1:
            a = _rope(a, cos_ref[...], sin_ref[...], quarter)
        o_ref[:, h * LANES:(h + 1) * LANES] = a.astype(o_ref.dtype)


def _mm_uq(x, w, rope, rows_per_batch, rope_dim, name, bm=1024, bn=1024):
    bm, bn = _blk(x.shape[0] if rope is None else rows_per_batch, bm), _blk(w.n_cols, bn)
    rargs, rspecs = _rope_specs(rope, bm, rows_per_batch)
    kern = functools.partial(_mm_uq_kernel, rope=rope is not None, quarter=rope_dim // 4)
    return _mm_call(kern, x, [w], bm, bn, rargs, rspecs, [BF16], name)[0]


def _mm_sigmoid_kernel(x_ref, w_ref, o_ref):
    o_ref[...] = _sigmoid(_xw(x_ref, w_ref)).astype(o_ref.dtype)


def _mm_sigmoid(x, w, out_dtype, name, bm=1024, bn=1024):
    bm, bn = _blk(x.shape[0], bm), _blk(w.n_cols, bn)
    return _mm_call(_mm_sigmoid_kernel, x, [w], bm, bn, [], [], [out_dtype], name)[0]


def _mm_swiglu_kernel(x_ref, wg_ref, wu_ref, o_ref):
    a = _xw(x_ref, wg_ref)
    b = _xw(x_ref, wu_ref)
    o_ref[...] = (a * _sigmoid(a) * b).astype(o_ref.dtype)


def _mm_swiglu(x, wg, wu, name, bm=1024, bn=256):
    bm, bn = _blk(x.shape[0], bm), _blk(wg.n_cols, bn)
    return _mm_call(_mm_swiglu_kernel, x, [wg, wu], bm, bn, [], [], [BF16], name)[0]


def _mm_merge_kernel(ya_ref, yb_ref, yc_ref, wa_ref, wb_ref, wc_ref, ga_ref, gb_ref, gc_ref, o_ref):
    a = jnp.dot(ya_ref[...], wa_ref[...], preferred_element_type=F32)
    b = jnp.dot(yb_ref[...], wb_ref[...], preferred_element_type=F32)
    c = jnp.dot(yc_ref[...], wc_ref[...], preferred_element_type=F32)
    ga, gb, gc = (g[...].astype(F32) for g in (ga_ref, gb_ref, gc_ref))
    o_ref[...] = (ga * a + gb * b + gc * c).astype(o_ref.dtype)


def _mm_merge(ya, yb, yc, wa, wb, wc, gates, name, bm=1024, bn=512):
    M = ya.shape[0]
    D = wa.shape[1]
    bm, bn = _blk(M, bm), _blk(D, bn)
    nb = D // bn
    xs = lambda y: pl.BlockSpec((bm, y.shape[1]), lambda i, j: (i, 0))
    wsp = lambda w: pl.BlockSpec((w.shape[0], bn), lambda i, j: (0, j))
    gsp = lambda t: pl.BlockSpec((bm, bn), lambda i, j: (i, j + t * nb))
    return pl.pallas_call(
        _mm_merge_kernel,
        grid=(M // bm, nb),
        in_specs=[xs(ya), xs(yb), xs(yc), wsp(wa), wsp(wb), wsp(wc), gsp(0), gsp(1), gsp(2)],
        out_specs=pl.BlockSpec((bm, bn), lambda i, j: (i, j)),
        out_shape=jax.ShapeDtypeStruct((M, D), BF16),
        compiler_params=_cparams(2),
        name=name,
    )(ya, yb, yc, wa, wb, wc, gates, gates, gates)


def _sgu_kernel(u_ref, v_ref, lg_ref, lb_ref, ws_ref, bias_ref, o_ref, *, chunk, groups):
    vn = _layer_norm(v_ref[...], lg_ref[...], lb_ref[...]).astype(BF16)
    gd = vn.shape[1] // groups
    for c in range(vn.shape[0] // chunk):
        r0 = c * chunk
        for g in range(groups):
            c0 = g * gd
            s = jnp.dot(ws_ref[g], vn[r0:r0 + chunk, c0:c0 + gd], preferred_element_type=F32)
            s = s + bias_ref[:, c0:c0 + gd]
            o_ref[r0:r0 + chunk, c0:c0 + gd] = (u_ref[r0:r0 + chunk, c0:c0 + gd] * s).astype(o_ref.dtype)


def _sgu(uv, ln_g, ln_b, w_s, bias_full, name):
    M = uv.shape[0]
    A = uv.shape[1] // 2
    G, C, _ = w_s.shape
    bt = _blk(M, 2 * C)
    return pl.pallas_call(
        functools.partial(_sgu_kernel, chunk=C, groups=G),
        grid=(M // bt,),
        in_specs=[
            pl.BlockSpec((bt, A), lambda i: (i, 0)),
            pl.BlockSpec((bt, A), lambda i: (i, 1)),
            pl.BlockSpec((1, A), lambda i: (0, 0)),
            pl.BlockSpec((1, A), lambda i: (0, 0)),
            pl.BlockSpec((G, C, C), lambda i: (0, 0, 0)),
            pl.BlockSpec((C, A), lambda i: (0, 0)),
        ],
        out_specs=pl.BlockSpec((bt, A), lambda i: (i, 0)),
        out_shape=jax.ShapeDtypeStruct((M, A), BF16),
        compiler_params=_cparams(1),
        name=name,
    )(uv, uv, ln_g.reshape(1, A), ln_b.reshape(1, A), w_s, bias_full)


_NT = (((1,), (1,)), ((), ()))
_LOG2E = 1.4426950408889634


def _attend(q, k, v, scale):
    s = lax.dot_general(q, k, _NT, preferred_element_type=F32)
    m = jnp.max(s, axis=-1, keepdims=True)
    p = jnp.exp2((s - m) * (scale * _LOG2E))
    l = jnp.sum(p, axis=-1, keepdims=True)
    return jnp.dot(p.astype(BF16), v, preferred_element_type=F32) / l


def _chunks(n_heads, rows, sub):
    out = [(h, r, sub) for h in range(n_heads) for r in range(0, rows, sub)]
    if sub >= 512 and len(out) >= 4:
        h, r, _ = out[0]
        out[0:1] = [(h, r, sub // 4), (h, r + sub // 4, sub - sub // 4)]
        h, r, _ = out[-1]
        out[-1:] = [(h, r, sub - sub // 4), (h, r + sub - sub // 4, sub // 4)]
    return out


def _gqa_kernel(q_ref, g_ref, *rest, nkv, group, hd, scale, sub, rope):
    if rope:
        cos_ref, sin_ref, k_ref, v_ref, o_ref = rest
    else:
        k_ref, v_ref, o_ref = rest
    for h, r, n in _chunks(nkv * group, q_ref.shape[0], sub):
        kv = h // group
        q = _rms(q_ref[r:r + n, h * hd:(h + 1) * hd], g_ref[...])
        if rope:
            q = _rope(q, cos_ref[r:r + n, :], sin_ref[r:r + n, :], hd // 4)
        o = _attend(q.astype(BF16), k_ref[:, kv * hd:(kv + 1) * hd],
                    v_ref[:, kv * hd:(kv + 1) * hd], scale)
        o_ref[r:r + n, h * hd:(h + 1) * hd] = o.astype(o_ref.dtype)


def _gqa(q, gain, rope, k, v, kv_heads, hd, name, bq, nkv, sub=512):
    B, Nq, W = q.shape
    S = k.shape[1]
    group = W // (kv_heads * hd)
    bq = _blk(Nq, bq)
    gw = nkv * group * hd
    sub = _blk(bq, sub)
    tab = pl.BlockSpec((bq, hd), lambda b, n, i: (i, 0))
    return pl.pallas_call(
        functools.partial(_gqa_kernel, nkv=nkv, group=group, hd=hd, scale=hd ** -0.5, sub=sub,
                          rope=rope is not None),
        grid=(B, kv_heads // nkv, Nq // bq),
        in_specs=[
            pl.BlockSpec((None, bq, gw), lambda b, n, i: (b, i, n)),
            pl.BlockSpec((1, hd), lambda b, n, i: (0, 0)),
        ] + ([tab, tab] if rope is not None else []) + [
            pl.BlockSpec((None, S, nkv * hd), lambda b, n, i: (b, 0, n)),
            pl.BlockSpec((None, S, nkv * hd), lambda b, n, i: (b, 0, n)),
        ],
        out_specs=pl.BlockSpec((None, bq, gw), lambda b, n, i: (b, i, n)),
        out_shape=jax.ShapeDtypeStruct((B, Nq, W), BF16),
        compiler_params=_cparams(3),
        name=name,
    )(q, gain.reshape(1, hd), *(rope if rope is not None else ()), k, v)


def _mla_kernel(q_ref, kn_ref, kpe_ref, v_ref, o_ref, *, hb, scale, sub):
    kpe = kpe_ref[...]
    ks = [jnp.concatenate([kn_ref[:, h * LANES:(h + 1) * LANES], kpe], axis=1) for h in range(hb)]
    for h, r, n in _chunks(hb, q_ref.shape[0], sub):
        o = _attend(q_ref[r:r + n, 2 * h * LANES:2 * (h + 1) * LANES], ks[h],
                    v_ref[:, h * LANES:(h + 1) * LANES], scale)
        o_ref[r:r + n, h * LANES:(h + 1) * LANES] = o.astype(o_ref.dtype)


def _mla(q, kv, kpe, heads, scale, name, bq, hb, sub=512):
    B, Nq, _ = q.shape
    S = kv.shape[1]
    bq = _blk(Nq, bq)
    sub = _blk(bq, sub)
    nhb = heads // hb
    return pl.pallas_call(
        functools.partial(_mla_kernel, hb=hb, scale=scale, sub=sub),
        grid=(B, nhb, Nq // bq),
        in_specs=[
            pl.BlockSpec((None, bq, 2 * hb * LANES), lambda b, h, i: (b, i, h)),
            pl.BlockSpec((None, S, hb * LANES), lambda b, h, i: (b, 0, h)),
            pl.BlockSpec((None, S, LANES), lambda b, h, i: (b, 0, 0)),
            pl.BlockSpec((None, S, hb * LANES), lambda b, h, i: (b, 0, nhb + h)),
        ],
        out_specs=pl.BlockSpec((None, bq, hb * LANES), lambda b, h, i: (b, i, h)),
        out_shape=jax.ShapeDtypeStruct((B, Nq, heads * LANES), BF16),
        compiler_params=_cparams(3),
        name=name,
    )(q, kv, kpe, kv)


def _rope_tables(n, dim):
    rows = n // GRID_W
    row = jnp.repeat(jnp.arange(rows, dtype=F32), GRID_W)
    col = jnp.tile(jnp.arange(GRID_W, dtype=F32), rows)
    quarter = dim // 4
    freqs = ROPE_THETA ** (-jnp.arange(quarter, dtype=F32) / quarter)
    ra = row[:, None] * freqs[None, :]
    ca = col[:, None] * freqs[None, :]
    ang = jnp.concatenate([ra, ra, ca, ca], axis=-1)
    sign = jnp.where((jnp.arange(dim) // quarter) % 2 == 0, -1.0, 1.0).astype(F32)
    cos = jnp.cos(ang)
    sin = jnp.sin(ang) * sign[None, :]
    if dim < LANES:
        cos = jnp.pad(cos, ((0, 0), (0, LANES - dim)), constant_values=1.0)
        sin = jnp.pad(sin, ((0, 0), (0, LANES - dim)))
    return cos, sin


def _layer_weights(l, dims, w_in, w_s, b_s, w_uq, w_ukv, w_pa, w_pb, w_pc, w_o, w_gate, w_up, w_down):
    A, qw, kvw, qr, rank, rd, D, mh, nope, vd = dims
    o = [0, 2 * A, 2 * A + qw, 2 * A + qw + 2 * kvw, 2 * A + qw + 2 * kvw + qr]
    o.append(o[-1] + rank + rd)
    bf = lambda a: a.astype(BF16)
    w = {}
    wi = w_in[l]
    w['uv'] = _wfull(bf(wi[:, o[0]:o[1]]))
    w['q'] = _wfull(bf(wi[:, o[1]:o[2]]))
    w['kv'] = _wfull(bf(wi[:, o[2]:o[3]]))
    w['cq'] = _wfull(bf(wi[:, o[3]:o[4]]))
    w['ckv'] = _wfull(bf(jnp.pad(wi[:, o[4]:o[5]], ((0, 0), (0, LANES - rd)))))
    w['g'] = _wfull(bf(wi[:, o[5]:]))
    w['o'] = _wfull(bf(w_o[l]))
    w['gate'] = _wfull(bf(w_gate[l]))
    w['up'] = _wfull(bf(w_up[l]))
    uq = w_uq[l].reshape(qr, mh, nope + rd)
    w['uq'] = _wfull(bf(jnp.pad(uq, ((0, 0), (0, 0), (0, 2 * LANES - nope - rd))).reshape(qr, mh * 2 * LANES)))
    ukv = w_ukv[l].reshape(rank, mh, nope + vd)
    w['ukv'] = _wfull(bf(jnp.concatenate([ukv[:, :, :nope].reshape(rank, mh * nope),
                                          ukv[:, :, nope:].reshape(rank, mh * vd)], axis=1)))
    w['pa'], w['pb'], w['pc'] = bf(w_pa[l]), bf(w_pb[l]), bf(w_pc[l])
    w['down'] = _wfull(bf(w_down[l]))
    w['s'] = bf(w_s[l])
    gd = A // w_s.shape[1]
    w['sb'] = jnp.repeat(b_s[l].T, gd, axis=1)
    return w


def _trunk_layer(x, h, mods, w, vecs, dims, B, T, rope, ctx, alpha, nxt):
    A, qw, kvw, qr, rank, rd, D, mh, nope, vd = dims
    sgu_g, sgu_b, qg, kg, cqg, ckvg, ln1g, ln1b, ln2g, ln2b = vecs
    M = B * T
    G = mods.shape[0]
    rpg = M // G
    hd = LANES
    kvh = kvw // hd
    tag = "s" if rope is not None else "p"
    rope_g = None if rope is None else rope[0]
    rope_m = None if rope is None else rope[1]

    uv = _mm_plain(h, w['uv'], F32, "proj_uv_" + tag)
    q = _mm_plain(h, w['q'], F32, "proj_q_" + tag)
    kv_f, kv_b = _mm_heads(h, w['kv'], kg, rope_g, T, kvh, [F32, BF16], "proj_kv_" + tag,
                           bm=1024, bn=2 * kvw)
    cq = _mm_rms(h, w['cq'], cqg, BF16, "proj_cq_" + tag)
    ckv_f, ckv_b, kpe_f, kpe_b = _mm_ckv(h, w['ckv'], ckvg, rope_m, T, rank, rd, "proj_ckv_" + tag)
    gates = _mm_sigmoid(h, w['g'], BF16, "proj_gates_" + tag)

    y_a = _sgu(uv, sgu_g, sgu_b, w['s'], w['sb'], "sgu_" + tag)

    k_b = kv_b[:, :kvw].reshape(B, T, kvw)
    v_b = kv_b[:, kvw:].reshape(B, T, kvw)
    if ctx is not None:
        c_k, c_v, c_ckv, c_kpe = ctx
        P = c_k.shape[1]
        k_b = jnp.concatenate([c_k.reshape(B, P, kvw).astype(BF16), k_b], axis=1)
        v_b = jnp.concatenate([c_v.reshape(B, P, kvw).astype(BF16), v_b], axis=1)
    long_keys = ctx is not None
    y_b = _gqa(q.reshape(B, T, qw), qg, rope_g, k_b, v_b, kvh, hd, "gqa_" + tag,
               bq=1024 if long_keys else 256, nkv=1 if long_keys else kvh).reshape(M, qw)

    qc = _mm_uq(cq, w['uq'], rope_m, T, rd, "mla_uq_" + tag)
    ckv_all = ckv_b.reshape(B, T, rank)
    kpe_all = kpe_b.reshape(B, T, LANES)
    if ctx is not None:
        ckv_all = jnp.concatenate([c_ckv.astype(BF16), ckv_all], axis=1)
        c_kpe_pad = jnp.pad(c_kpe, ((0, 0), (0, 0), (0, LANES - rd))).astype(BF16)
        kpe_all = jnp.concatenate([c_kpe_pad, kpe_all], axis=1)
    S = ckv_all.shape[1]
    kvu = _mm_plain(ckv_all.reshape(B * S, rank), w['ukv'], BF16, "mla_ukv_" + tag)
    y_c = _mla(qc.reshape(B, T, mh * 2 * LANES), kvu.reshape(B, S, mh * (nope + vd)), kpe_all,
               mh, (nope + rd) ** -0.5, "mla_" + tag,
               bq=4096 if long_keys else 256, hb=1 if long_keys else mh).reshape(M, mh * vd)

    merged = _mm_merge(y_a, y_b, y_c, w['pa'], w['pb'], w['pc'], gates, "merge_" + tag)
    mix = _mm_plain(merged, w['o'], F32, "proj_o_" + tag)
    x1, h2 = _ln_residual(x, mix, mods, 2, ln1g, ln1b, alpha, rpg, nxt=(mods, 4, 3))
    hid = _mm_swiglu(h2, w['gate'], w['up'], "ffn_in_" + tag)
    ff = _mm_plain(hid, w['down'], F32, "ffn_out_" + tag, bm=512, bn=512)
    x2, h_next = _ln_residual(x1, ff, mods, 5, ln2g, ln2b, alpha, rpg, nxt=nxt)

    own = (kv_f[:, :kvw], kv_f[:, kvw:], ckv_f, kpe_f[:, :rd])
    return x2, h_next, own


def kernel(x_prompt, x_sample, cache_k, cache_v, cache_ckv, cache_kpe, c, c_ctx,
           w_ada, b_ada, w_in, sgu_ln_g, sgu_ln_b, w_s, b_s, q_norm_g, k_norm_g,
           mla_q_norm_g, mla_kv_norm_g, w_uq, w_ukv, w_pa, w_pb, w_pc, w_o,
           ln1_g, ln1_b, ln2_g, ln2_b, w_gate, w_up, w_down):
    Bp, Tp, D = x_prompt.shape
    Bs, Ts, _ = x_sample.shape
    L = w_ada.shape[0]
    A = sgu_ln_g.shape[1]
    hd = q_norm_g.shape[1]
    kvh = cache_k.shape[3]
    qw = w_pb.shape[1]
    kvw = kvh * hd
    qr = mla_q_norm_g.shape[1]
    rank = mla_kv_norm_g.shape[1]
    rd = cache_kpe.shape[-1]
    vd = LANES
    mh = w_pc.shape[1] // vd
    nope = w_uq.shape[2] // mh - rd
    assert hd == LANES and nope == LANES and w_ukv.shape[2] == mh * (nope + vd)
    dims = (A, qw, kvw, qr, rank, rd, D, mh, nope, vd)
    alpha = float((2 * L) ** 0.25)

    R = -(-(1 + Bs) // 8) * 8
    cond = jnp.concatenate([c_ctx[None, :], c, jnp.zeros((R - 1 - Bs, D), F32)], axis=0)
    mods_all = _ada(cond, w_ada, b_ada)

    rope = (_rope_tables(Ts, hd), _rope_tables(Ts, rd))

    xp = x_prompt.reshape(Bp * Tp, D)
    xs = x_sample.reshape(Bs * Ts, D)
    mods = [(mods_all[l, 0:1].reshape(1, 1, 6 * D), mods_all[l, 1:1 + Bs].reshape(Bs, 1, 6 * D))
            for l in range(L)]
    hp = _modulate(xp, mods[0][0], 1, 0, Bp * Tp)
    hs = _modulate(xs, mods[0][1], 1, 0, Ts)

    new_k, new_v, new_ckv, new_kpe = [], [], [], []
    for l in range(L):
        w = _layer_weights(l, dims, w_in, w_s, b_s, w_uq, w_ukv, w_pa, w_pb, w_pc, w_o, w_gate, w_up, w_down)
        vecs = (sgu_ln_g[l], sgu_ln_b[l], q_norm_g[l], k_norm_g[l], mla_q_norm_g[l], mla_kv_norm_g[l],
                ln1_g[l], ln1_b[l], ln2_g[l], ln2_b[l])
        nxt_p = (mods[l + 1][0], 1, 0) if l + 1 < L else None
        nxt_s = (mods[l + 1][1], 1, 0) if l + 1 < L else None
        xp, hp, own = _trunk_layer(xp, hp, mods[l][0], w, vecs, dims, Bp, Tp, None, None, alpha, nxt_p)
        new_k.append(own[0].reshape(Bp, Tp, kvh, hd))
        new_v.append(own[1].reshape(Bp, Tp, kvh, hd))
        new_ckv.append(own[2].reshape(Bp, Tp, rank))
        new_kpe.append(own[3].reshape(Bp, Tp, rd))
        ctx = (cache_k[:, l], cache_v[:, l], cache_ckv[:, l], cache_kpe[:, l])
        xs, hs, _ = _trunk_layer(xs, hs, mods[l][1], w, vecs, dims, Bs, Ts, rope, ctx, alpha, nxt_s)

    return (xp.reshape(Bp, Tp, D), xs.reshape(Bs, Ts, D),
            jnp.stack(new_k, axis=1), jnp.stack(new_v, axis=1),
            jnp.stack(new_ckv, axis=1), jnp.stack(new_kpe, axis=1))
```

```python
import functools
from typing import NamedTuple

import jax
import jax.numpy as jnp
from jax import lax
from jax.experimental import pallas as pl
from jax.experimental.pallas import tpu as pltpu

F32 = jnp.float32
BF16 = jnp.bfloat16

NORM_EPS = 1e-6
ROPE_THETA = 10000.0
GRID_W = 64

LANES = 128
VMEM_LIMIT_BYTES = 56 * 1024 * 1024


def _cparams(n_axes):
    return pltpu.CompilerParams(
        dimension_semantics=("arbitrary",) * n_axes,
        vmem_limit_bytes=VMEM_LIMIT_BYTES,
    )


def _blk(n, pref):
    b = min(n, pref)
    while n % b:
        b //= 2
    return b


def _rms(a, g):
    ms = jnp.mean(a * a, axis=-1, keepdims=True)
    return a * lax.rsqrt(ms + NORM_EPS) * g


def _rope(y, cos, sin_signed, quarter):
    n = y.shape[-1]
    lane = lax.broadcasted_iota(jnp.int32, y.shape, 1)
    first = (lane & quarter) == 0
    rot = jnp.where(first, pltpu.roll(y, n - quarter, axis=1), pltpu.roll(y, quarter, axis=1))
    return y * cos + rot * sin_signed


def _sigmoid(x):
    return 1.0 / (1.0 + jnp.exp(-x))


def _layer_norm(z, g, b):
    mu = jnp.mean(z, axis=-1, keepdims=True)
    zc = z - mu
    var = jnp.mean(zc * zc, axis=-1, keepdims=True)
    return zc * lax.rsqrt(var + NORM_EPS) * g + b


def _ada_kernel(c_ref, w_ref, b_ref, o_ref):
    c = c_ref[...]
    s = (c * _sigmoid(c)).astype(BF16)
    o_ref[...] = jnp.dot(s, w_ref[...].astype(BF16), preferred_element_type=F32) + b_ref[...]


def _ada(cond, w_ada, b_ada):
    L, D, N = w_ada.shape
    R = cond.shape[0]
    bn = _blk(N, 512)
    return pl.pallas_call(
        _ada_kernel,
        grid=(L, N // bn),
        in_specs=[
            pl.BlockSpec((R, D), lambda l, j: (0, 0)),
            pl.BlockSpec((None, D, bn), lambda l, j: (l, 0, j)),
            pl.BlockSpec((None, 1, bn), lambda l, j: (l, 0, j)),
        ],
        out_specs=pl.BlockSpec((None, R, bn), lambda l, j: (l, 0, j)),
        out_shape=jax.ShapeDtypeStruct((L, R, N), F32),
        compiler_params=_cparams(2),
        name="ada_mod",
    )(cond, w_ada, b_ada.reshape(L, 1, N))


def _modulate_kernel(x_ref, sc_ref, sh_ref, h_ref):
    h_ref[...] = (x_ref[...] * (1.0 + sc_ref[...]) + sh_ref[...]).astype(h_ref.dtype)


def _mod_spec(D, k, rows_per_group, bt):
    return pl.BlockSpec((None, 1, D), lambda i, *_: ((i * bt) // rows_per_group, 0, k))


def _modulate(x, mods, k_sc, k_sh, rows_per_group):
    M, D = x.shape
    bt = _blk(rows_per_group, 512)
    return pl.pallas_call(
        _modulate_kernel,
        grid=(M // bt,),
        in_specs=[
            pl.BlockSpec((bt, D), lambda i: (i, 0)),
            _mod_spec(D, k_sc, rows_per_group, bt),
            _mod_spec(D, k_sh, rows_per_group, bt),
        ],
        out_specs=pl.BlockSpec((bt, D), lambda i: (i, 0)),
        out_shape=jax.ShapeDtypeStruct((M, D), BF16),
        compiler_params=_cparams(1),
        name="modulate",
    )(x, mods, mods)


def _ln_kernel(x_ref, y_ref, gate_ref, g_ref, b_ref, *rest, alpha, with_h):
    z = alpha * x_ref[...] + gate_ref[...] * y_ref[...].astype(F32)
    xn = _layer_norm(z, g_ref[...], b_ref[...])
    if with_h:
        sc_ref, sh_ref, xo_ref, h_ref = rest
        xo_ref[...] = xn
        h_ref[...] = (xn * (1.0 + sc_ref[...]) + sh_ref[...]).astype(h_ref.dtype)
    else:
        (xo_ref,) = rest
        xo_ref[...] = xn


def _ln_residual(x, y, mods, k_gate, ln_g, ln_b, alpha, rows_per_group, nxt=None):
    M, D = x.shape
    bt = _blk(rows_per_group, 256)
    row = pl.BlockSpec((bt, D), lambda i: (i, 0))
    vec = pl.BlockSpec((1, D), lambda i: (0, 0))
    in_specs = [row, row, _mod_spec(D, k_gate, rows_per_group, bt), vec, vec]
    args = [x, y, mods, ln_g.reshape(1, D), ln_b.reshape(1, D)]
    out_shape = [jax.ShapeDtypeStruct((M, D), F32)]
    out_specs = [row]
    if nxt is not None:
        mods_n, k_sc, k_sh = nxt
        in_specs += [_mod_spec(D, k_sc, rows_per_group, bt), _mod_spec(D, k_sh, rows_per_group, bt)]
        args += [mods_n, mods_n]
        out_shape.append(jax.ShapeDtypeStruct((M, D), BF16))
        out_specs.append(row)
    res = pl.pallas_call(
        functools.partial(_ln_kernel, alpha=alpha, with_h=nxt is not None),
        grid=(M // bt,),
        in_specs=in_specs,
        out_specs=out_specs,
        out_shape=out_shape,
        compiler_params=_cparams(1),
        name="ln_residual",
    )(*args)
    return res if nxt is not None else (res[0], None)


class _W(NamedTuple):
    arr: jax.Array
    layer: int
    col0: int
    n_cols: int


def _wfull(a):
    return _W(a[None], 0, 0, a.shape[1])


def _wspec(w, bn):
    K = w.arr.shape[1]
    l, ob = w.layer, w.col0 // bn
    assert w.col0 % bn == 0 and w.n_cols % bn == 0
    return pl.BlockSpec((None, K, bn), lambda i, j: (l, 0, ob + j))


def _mm_call(kernel, x, ws, bm, bn, extras, extra_specs, out_dtypes, name):
    M, K = x.shape
    n_cols = ws[0].n_cols
    in_specs = [pl.BlockSpec((bm, K), lambda i, j: (i, 0))]
    in_specs += [_wspec(w, bn) for w in ws]
    in_specs += list(extra_specs)
    out_shape = [jax.ShapeDtypeStruct((M, n_cols), dt) for dt in out_dtypes]
    out_specs = [pl.BlockSpec((bm, bn), lambda i, j: (i, j)) for _ in out_dtypes]
    return pl.pallas_call(
        kernel,
        grid=(M // bm, n_cols // bn),
        in_specs=in_specs,
        out_specs=out_specs,
        out_shape=out_shape,
        compiler_params=_cparams(2),
        name=name,
    )(x, *[w.arr for w in ws], *extras)


def _xw(x_ref, w_ref):
    return jnp.dot(x_ref[...], w_ref[...].astype(BF16), preferred_element_type=F32)


def _mm_plain_kernel(x_ref, w_ref, o_ref):
    o_ref[...] = _xw(x_ref, w_ref).astype(o_ref.dtype)


def _mm_plain(x, w, out_dtype, name, bm=1024, bn=1024):
    bm, bn = _blk(x.shape[0], bm), _blk(w.n_cols, bn)
    return _mm_call(_mm_plain_kernel, x, [w], bm, bn, [], [], [out_dtype], name)[0]


def _rope_specs(rope, bm, rows_per_batch):
    if rope is None:
        return [], []
    nb = rows_per_batch // bm
    spec = pl.BlockSpec((bm, LANES), lambda i, j: (i % nb, 0))
    return [rope[0], rope[1]], [spec, spec]


def _mm_heads_kernel(x_ref, w_ref, g_ref, *rest, n_norm, rope, quarter):
    if rope:
        cos_ref, sin_ref, *outs = rest
    else:
        outs = rest
    acc = _xw(x_ref, w_ref)
    for h in range(acc.shape[1] // LANES):
        a = acc[:, h * LANES:(h + 1) * LANES]
        if h < n_norm:
            a = _rms(a, g_ref[...])
            if rope:
                a = _rope(a, cos_ref[...], sin_ref[...], quarter)
        for o in outs:
            o[:, h * LANES:(h + 1) * LANES] = a.astype(o.dtype)


def _mm_heads(x, w, gain, rope, rows_per_batch, n_norm, out_dtypes, name, bm, bn):
    bm, bn = _blk(x.shape[0] if rope is None else rows_per_batch, bm), _blk(w.n_cols, bn)
    rargs, rspecs = _rope_specs(rope, bm, rows_per_batch)
    kern = functools.partial(_mm_heads_kernel, n_norm=n_norm, rope=rope is not None,
                             quarter=LANES // 4)
    return _mm_call(kern, x, [w], bm, bn,
                    [gain.reshape(1, LANES)] + rargs,
                    [pl.BlockSpec((1, LANES), lambda i, j: (0, 0))] + rspecs,
                    out_dtypes, name)


def _mm_rms_kernel(x_ref, w_ref, g_ref, o_ref):
    o_ref[...] = _rms(_xw(x_ref, w_ref), g_ref[...]).astype(o_ref.dtype)


def _mm_rms(x, w, gain, out_dtype, name, bm=1024):
    N = w.n_cols
    bm = _blk(x.shape[0], bm)
    return _mm_call(_mm_rms_kernel, x, [w], bm, N, [gain.reshape(1, N)],
                    [pl.BlockSpec((1, N), lambda i, j: (0, 0))], [out_dtype], name)[0]


def _mm_ckv_kernel(x_ref, w_ref, g_ref, *rest, rank, rope_dim, rope):
    if rope:
        cos_ref, sin_ref, ckv_f, ckv_b, kpe_f, kpe_b = rest
    else:
        ckv_f, ckv_b, kpe_f, kpe_b = rest
    acc = jnp.dot(x_ref[...], w_ref[:, :rank + LANES].astype(BF16), preferred_element_type=F32)
    ckv = _rms(acc[:, :rank], g_ref[...])
    kpe = acc[:, rank:rank + LANES]
    lane = lax.broadcasted_iota(jnp.int32, kpe.shape, 1)
    kpe = jnp.where(lane < rope_dim, kpe, 0.0)
    if rope:
        kpe = _rope(kpe, cos_ref[...], sin_ref[...], rope_dim // 4)
    ckv_f[...] = ckv
    ckv_b[...] = ckv.astype(BF16)
    kpe_f[...] = kpe
    kpe_b[...] = kpe.astype(BF16)


def _mm_ckv(x, w, gain, rope, rows_per_batch, rank, rope_dim, name, bm=1024):
    M, K = x.shape
    bw = w.n_cols
    assert bw >= rank + LANES and w.col0 % bw == 0
    bm = _blk(M if rope is None else rows_per_batch, bm)
    rargs, rspecs = _rope_specs(rope, bm, rows_per_batch)
    kern = functools.partial(_mm_ckv_kernel, rank=rank, rope_dim=rope_dim, rope=rope is not None)
    in_specs = [pl.BlockSpec((bm, K), lambda i, j: (i, 0)),
                _wspec(w, bw),
                pl.BlockSpec((1, rank), lambda i, j: (0, 0))] + rspecs
    out_shape = [jax.ShapeDtypeStruct((M, rank), F32), jax.ShapeDtypeStruct((M, rank), BF16),
                 jax.ShapeDtypeStruct((M, LANES), F32), jax.ShapeDtypeStruct((M, LANES), BF16)]
    out_specs = [pl.BlockSpec((bm, rank), lambda i, j: (i, 0)), pl.BlockSpec((bm, rank), lambda i, j: (i, 0)),
                 pl.BlockSpec((bm, LANES), lambda i, j: (i, 0)), pl.BlockSpec((bm, LANES), lambda i, j: (i, 0))]
    return pl.pallas_call(
        kern, grid=(M // bm, 1), in_specs=in_specs, out_specs=out_specs, out_shape=out_shape,
        compiler_params=_cparams(2), name=name,
    )(x, w.arr, gain.reshape(1, rank), *rargs)


def _mm_uq_kernel(x_ref, w_ref, *rest, rope, quarter):
    if rope:
        cos_ref, sin_ref, o_ref = rest
    else:
        (o_ref,) = rest
    acc = _xw(x_ref, w_ref)
    for h in range(acc.shape[1] // LANES):
        a = acc[:, h * LANES:(h + 1) * LANES]
        if rope and h % 2 == 1:
            a = _rope(a, cos_ref[...], sin_ref[...], quarter)
        o_ref[:, h * LANES:(h + 1) * LANES] = a.astype(o_ref.dtype)


def _mm_uq(x, w, rope, rows_per_batch, rope_dim, name, bm=1024, bn=1024):
    bm, bn = _blk(x.shape[0] if rope is None else rows_per_batch, bm), _blk(w.n_cols, bn)
    rargs, rspecs = _rope_specs(rope, bm, rows_per_batch)
    kern = functools.partial(_mm_uq_kernel, rope=rope is not None, quarter=rope_dim // 4)
    return _mm_call(kern, x, [w], bm, bn, rargs, rspecs, [BF16], name)[0]


def _mm_sigmoid_kernel(x_ref, w_ref, o_ref):
    o_ref[...] = _sigmoid(_xw(x_ref, w_ref)).astype(o_ref.dtype)


def _mm_sigmoid(x, w, out_dtype, name, bm=1024, bn=1024):
    bm, bn = _blk(x.shape[0], bm), _blk(w.n_cols, bn)
    return _mm_call(_mm_sigmoid_kernel, x, [w], bm, bn, [], [], [out_dtype], name)[0]


def _mm_swiglu_kernel(x_ref, wg_ref, wu_ref, o_ref):
    a = _xw(x_ref, wg_ref)
    b = _xw(x_ref, wu_ref)
    o_ref[...] = (a * _sigmoid(a) * b).astype(o_ref.dtype)


def _mm_swiglu(x, wg, wu, name, bm=1024, bn=256):
    bm, bn = _blk(x.shape[0], bm), _blk(wg.n_cols, bn)
    return _mm_call(_mm_swiglu_kernel, x, [wg, wu], bm, bn, [], [], [BF16], name)[0]


def _mm_merge_kernel(ya_ref, yb_ref, yc_ref, wa_ref, wb_ref, wc_ref, ga_ref, gb_ref, gc_ref, o_ref):
    a = jnp.dot(ya_ref[...], wa_ref[...], preferred_element_type=F32)
    b = jnp.dot(yb_ref[...], wb_ref[...], preferred_element_type=F32)
    c = jnp.dot(yc_ref[...], wc_ref[...], preferred_element_type=F32)
    ga, gb, gc = (g[...].astype(F32) for g in (ga_ref, gb_ref, gc_ref))
    o_ref[...] = (ga * a + gb * b + gc * c).astype(o_ref.dtype)


def _mm_merge(ya, yb, yc, wa, wb, wc, gates, name, bm=1024, bn=512):
    M = ya.shape[0]
    D = wa.n_cols
    bm, bn = _blk(M, bm), _blk(D, bn)
    nb = D // bn
    xs = lambda y: pl.BlockSpec((bm, y.shape[1]), lambda i, j: (i, 0))
    gsp = lambda t: pl.BlockSpec((bm, bn), lambda i, j: (i, j + t * nb))
    return pl.pallas_call(
        _mm_merge_kernel,
        grid=(M // bm, nb),
        in_specs=[xs(ya), xs(yb), xs(yc), _wspec(wa, bn), _wspec(wb, bn), _wspec(wc, bn),
                  gsp(0), gsp(1), gsp(2)],
        out_specs=pl.BlockSpec((bm, bn), lambda i, j: (i, j)),
        out_shape=jax.ShapeDtypeStruct((M, D), BF16),
        compiler_params=_cparams(2),
        name=name,
    )(ya, yb, yc, wa.arr, wb.arr, wc.arr, gates, gates, gates)


def _sgu_kernel(u_ref, v_ref, lg_ref, lb_ref, ws_ref, bias_ref, o_ref, *, chunk, groups):
    vn = _layer_norm(v_ref[...], lg_ref[...], lb_ref[...]).astype(BF16)
    gd = vn.shape[1] // groups
    for c in range(vn.shape[0] // chunk):
        r0 = c * chunk
        for g in range(groups):
            c0 = g * gd
            s = jnp.dot(ws_ref[g], vn[r0:r0 + chunk, c0:c0 + gd], preferred_element_type=F32)
            s = s + bias_ref[:, c0:c0 + gd]
            o_ref[r0:r0 + chunk, c0:c0 + gd] = (u_ref[r0:r0 + chunk, c0:c0 + gd] * s).astype(o_ref.dtype)


def _sgu(uv, ln_g, ln_b, w_s, bias_full, name):
    M = uv.shape[0]
    A = uv.shape[1] // 2
    G, C, _ = w_s.shape
    bt = _blk(M, 2 * C)
    return pl.pallas_call(
        functools.partial(_sgu_kernel, chunk=C, groups=G),
        grid=(M // bt,),
        in_specs=[
            pl.BlockSpec((bt, A), lambda i: (i, 0)),
            pl.BlockSpec((bt, A), lambda i: (i, 1)),
            pl.BlockSpec((1, A), lambda i: (0, 0)),
            pl.BlockSpec((1, A), lambda i: (0, 0)),
            pl.BlockSpec((G, C, C), lambda i: (0, 0, 0)),
            pl.BlockSpec((C, A), lambda i: (0, 0)),
        ],
        out_specs=pl.BlockSpec((bt, A), lambda i: (i, 0)),
        out_shape=jax.ShapeDtypeStruct((M, A), BF16),
        compiler_params=_cparams(1),
        name=name,
    )(uv, uv, ln_g.reshape(1, A), ln_b.reshape(1, A), w_s, bias_full)


_NT = (((1,), (1,)), ((), ()))
_LOG2E = 1.4426950408889634


def _attend(q, k, v, scale):
    s = lax.dot_general(q, k, _NT, preferred_element_type=F32)
    m = jnp.max(s, axis=-1, keepdims=True)
    p = jnp.exp2((s - m) * (scale * _LOG2E))
    l = jnp.sum(p, axis=-1, keepdims=True)
    return jnp.dot(p.astype(BF16), v, preferred_element_type=F32) / l


def _chunks(n_heads, rows, sub):
    out = [(h, r, sub) for h in range(n_heads) for r in range(0, rows, sub)]
    if sub >= 512 and len(out) >= 4:
        h, r, _ = out[0]
        out[0:1] = [(h, r, sub // 4), (h, r + sub // 4, sub - sub // 4)]
        h, r, _ = out[-1]
        out[-1:] = [(h, r, sub - sub // 4), (h, r + sub - sub // 4, sub // 4)]
    return out


def _gqa_kernel(q_ref, g_ref, *rest, nkv, group, hd, scale, sub, rope):
    if rope:
        cos_ref, sin_ref, k_ref, v_ref, o_ref = rest
    else:
        k_ref, v_ref, o_ref = rest
    for h, r, n in _chunks(nkv * group, q_ref.shape[0], sub):
        kv = h // group
        q = _rms(q_ref[r:r + n, h * hd:(h + 1) * hd], g_ref[...])
        if rope:
            q = _rope(q, cos_ref[r:r + n, :], sin_ref[r:r + n, :], hd // 4)
        o = _attend(q.astype(BF16), k_ref[:, kv * hd:(kv + 1) * hd],
                    v_ref[:, kv * hd:(kv + 1) * hd], scale)
        o_ref[r:r + n, h * hd:(h + 1) * hd] = o.astype(o_ref.dtype)


def _gqa(q, gain, rope, k, v, kv_heads, hd, name, bq, nkv, sub=512):
    B, Nq, W = q.shape
    S = k.shape[1]
    group = W // (kv_heads * hd)
    bq = _blk(Nq, bq)
    gw = nkv * group * hd
    sub = _blk(bq, sub)
    tab = pl.BlockSpec((bq, hd), lambda b, n, i: (i, 0))
    return pl.pallas_call(
        functools.partial(_gqa_kernel, nkv=nkv, group=group, hd=hd, scale=hd ** -0.5, sub=sub,
                          rope=rope is not None),
        grid=(B, kv_heads // nkv, Nq // bq),
        in_specs=[
            pl.BlockSpec((None, bq, gw), lambda b, n, i: (b, i, n)),
            pl.BlockSpec((1, hd), lambda b, n, i: (0, 0)),
        ] + ([tab, tab] if rope is not None else []) + [
            pl.BlockSpec((None, S, nkv * hd), lambda b, n, i: (b, 0, n)),
            pl.BlockSpec((None, S, nkv * hd), lambda b, n, i: (b, 0, n)),
        ],
        out_specs=pl.BlockSpec((None, bq, gw), lambda b, n, i: (b, i, n)),
        out_shape=jax.ShapeDtypeStruct((B, Nq, W), BF16),
        compiler_params=_cparams(3),
        name=name,
    )(q, gain.reshape(1, hd), *(rope if rope is not None else ()), k, v)


def _mla_kernel(q_ref, kn_ref, kpe_ref, v_ref, o_ref, *, hb, scale, sub):
    kpe = kpe_ref[...]
    ks = [jnp.concatenate([kn_ref[:, h * LANES:(h + 1) * LANES], kpe], axis=1) for h in range(hb)]
    for h, r, n in _chunks(hb, q_ref.shape[0], sub):
        o = _attend(q_ref[r:r + n, 2 * h * LANES:2 * (h + 1) * LANES], ks[h],
                    v_ref[:, h * LANES:(h + 1) * LANES], scale)
        o_ref[r:r + n, h * LANES:(h + 1) * LANES] = o.astype(o_ref.dtype)


def _mla(q, kv, kpe, heads, scale, name, bq, hb, sub=512):
    B, Nq, _ = q.shape
    S = kv.shape[1]
    bq = _blk(Nq, bq)
    sub = _blk(bq, sub)
    nhb = heads // hb
    return pl.pallas_call(
        functools.partial(_mla_kernel, hb=hb, scale=scale, sub=sub),
        grid=(B, nhb, Nq // bq),
        in_specs=[
            pl.BlockSpec((None, bq, 2 * hb * LANES), lambda b, h, i: (b, i, h)),
            pl.BlockSpec((None, S, hb * LANES), lambda b, h, i: (b, 0, h)),
            pl.BlockSpec((None, S, LANES), lambda b, h, i: (b, 0, 0)),
            pl.BlockSpec((None, S, hb * LANES), lambda b, h, i: (b, 0, nhb + h)),
        ],
        out_specs=pl.BlockSpec((None, bq, hb * LANES), lambda b, h, i: (b, i, h)),
        out_shape=jax.ShapeDtypeStruct((B, Nq, heads * LANES), BF16),
        compiler_params=_cparams(3),
        name=name,
    )(q, kv, kpe, kv)


def _rope_tables(n, dim):
    rows = n // GRID_W
    row = jnp.repeat(jnp.arange(rows, dtype=F32), GRID_W)
    col = jnp.tile(jnp.arange(GRID_W, dtype=F32), rows)
    quarter = dim // 4
    freqs = ROPE_THETA ** (-jnp.arange(quarter, dtype=F32) / quarter)
    ra = row[:, None] * freqs[None, :]
    ca = col[:, None] * freqs[None, :]
    ang = jnp.concatenate([ra, ra, ca, ca], axis=-1)
    sign = jnp.where((jnp.arange(dim) // quarter) % 2 == 0, -1.0, 1.0).astype(F32)
    cos = jnp.cos(ang)
    sin = jnp.sin(ang) * sign[None, :]
    if dim < LANES:
        cos = jnp.pad(cos, ((0, 0), (0, LANES - dim)), constant_values=1.0)
        sin = jnp.pad(sin, ((0, 0), (0, LANES - dim)))
    return cos, sin


GATE_ALIGN = 1024


def _stacked_weights(dims, w_in, w_s, b_s, w_uq, w_ukv, w_pa, w_pb, w_pc, w_o, w_gate, w_up, w_down):
    A, qw, kvw, qr, rank, rd, D, mh, nope, vd = dims
    L = w_in.shape[0]
    bf = lambda a: a.astype(BF16)
    o_g = 2 * A + qw + 2 * kvw + qr + rank + rd
    pad = (-o_g) % GATE_ALIGN
    s = {}
    s['in'] = bf(jnp.concatenate([w_in[..., :o_g], jnp.zeros((L, D, pad), w_in.dtype), w_in[..., o_g:]], axis=-1))
    s['g0'] = o_g + pad
    uq = w_uq.reshape(L, qr, mh, nope + rd)
    s['uq'] = bf(jnp.pad(uq, ((0, 0), (0, 0), (0, 0), (0, 2 * LANES - nope - rd))).reshape(L, qr, mh * 2 * LANES))
    ukv = w_ukv.reshape(L, rank, mh, nope + vd)
    s['ukv'] = bf(jnp.concatenate([ukv[..., :nope].reshape(L, rank, mh * nope),
                                   ukv[..., nope:].reshape(L, rank, mh * vd)], axis=-1))
    for name, a in (('pa', w_pa), ('pb', w_pb), ('pc', w_pc), ('o', w_o), ('gate', w_gate),
                    ('up', w_up), ('down', w_down), ('s', w_s)):
        s[name] = bf(a)
    gd = A // w_s.shape[1]
    s['sb'] = jnp.repeat(jnp.swapaxes(b_s, 1, 2), gd, axis=2)
    return s


def _layer_weights(l, dims, s):
    A, qw, kvw, qr, rank, rd, D, mh, nope, vd = dims
    o = [0, 2 * A, 2 * A + qw, 2 * A + qw + 2 * kvw, 2 * A + qw + 2 * kvw + qr]
    full = lambda a: _W(a, l, 0, a.shape[2])
    w = {}
    w['uv'] = _W(s['in'], l, o[0], 2 * A)
    w['q'] = _W(s['in'], l, o[1], qw)
    w['kv'] = _W(s['in'], l, o[2], 2 * kvw)
    w['cq'] = _W(s['in'], l, o[3], qr)
    w['ckv'] = _W(s['in'], l, o[4], s['g0'] - o[4])
    w['g'] = _W(s['in'], l, s['g0'], s['in'].shape[2] - s['g0'])
    for name in ('uq', 'ukv', 'pa', 'pb', 'pc', 'o', 'gate', 'up', 'down'):
        w[name] = full(s[name])
    w['s'] = s['s'][l]
    w['sb'] = s['sb'][l]
    return w


def _trunk_layer(x, h, mods, w, vecs, dims, B, T, rope, ctx, alpha, nxt):
    A, qw, kvw, qr, rank, rd, D, mh, nope, vd = dims
    sgu_g, sgu_b, qg, kg, cqg, ckvg, ln1g, ln1b, ln2g, ln2b = vecs
    M = B * T
    G = mods.shape[0]
    rpg = M // G
    hd = LANES
    kvh = kvw // hd
    tag = "s" if rope is not None else "p"
    rope_g = None if rope is None else rope[0]
    rope_m = None if rope is None else rope[1]

    uv = _mm_plain(h, w['uv'], F32, "proj_uv_" + tag)
    q = _mm_plain(h, w['q'], F32, "proj_q_" + tag)
    kv_f, kv_b = _mm_heads(h, w['kv'], kg, rope_g, T, kvh, [F32, BF16], "proj_kv_" + tag,
                           bm=1024, bn=2 * kvw)
    cq = _mm_rms(h, w['cq'], cqg, BF16, "proj_cq_" + tag)
    ckv_f, ckv_b, kpe_f, kpe_b = _mm_ckv(h, w['ckv'], ckvg, rope_m, T, rank, rd, "proj_ckv_" + tag)
    gates = _mm_sigmoid(h, w['g'], BF16, "proj_gates_" + tag)

    y_a = _sgu(uv, sgu_g, sgu_b, w['s'], w['sb'], "sgu_" + tag)

    k_b = kv_b[:, :kvw].reshape(B, T, kvw)
    v_b = kv_b[:, kvw:].reshape(B, T, kvw)
    if ctx is not None:
        c_k, c_v, c_ckv, c_kpe = ctx
        P = c_k.shape[1]
        k_b = jnp.concatenate([c_k.reshape(B, P, kvw).astype(BF16), k_b], axis=1)
        v_b = jnp.concatenate([c_v.reshape(B, P, kvw).astype(BF16), v_b], axis=1)
    long_keys = ctx is not None
    y_b = _gqa(q.reshape(B, T, qw), qg, rope_g, k_b, v_b, kvh, hd, "gqa_" + tag,
               bq=1024 if long_keys else 256, nkv=1 if long_keys else kvh).reshape(M, qw)

    qc = _mm_uq(cq, w['uq'], rope_m, T, rd, "mla_uq_" + tag)
    ckv_all = ckv_b.reshape(B, T, rank)
    kpe_all = kpe_b.reshape(B, T, LANES)
    if ctx is not None:
        ckv_all = jnp.concatenate([c_ckv.astype(BF16), ckv_all], axis=1)
        c_kpe_pad = jnp.pad(c_kpe, ((0, 0), (0, 0), (0, LANES - rd))).astype(BF16)
        kpe_all = jnp.concatenate([c_kpe_pad, kpe_all], axis=1)
    S = ckv_all.shape[1]
    kvu = _mm_plain(ckv_all.reshape(B * S, rank), w['ukv'], BF16, "mla_ukv_" + tag)
    y_c = _mla(qc.reshape(B, T, mh * 2 * LANES), kvu.reshape(B, S, mh * (nope + vd)), kpe_all,
               mh, (nope + rd) ** -0.5, "mla_" + tag,
               bq=4096 if long_keys else 256, hb=1 if long_keys else mh).reshape(M, mh * vd)

    merged = _mm_merge(y_a, y_b, y_c, w['pa'], w['pb'], w['pc'], gates, "merge_" + tag)
    mix = _mm_plain(merged, w['o'], BF16, "proj_o_" + tag)
    x1, h2 = _ln_residual(x, mix, mods, 2, ln1g, ln1b, alpha, rpg, nxt=(mods, 4, 3))
    hid = _mm_swiglu(h2, w['gate'], w['up'], "ffn_in_" + tag)
    ff = _mm_plain(hid, w['down'], BF16, "ffn_out_" + tag, bm=512, bn=512)
    x2, h_next = _ln_residual(x1, ff, mods, 5, ln2g, ln2b, alpha, rpg, nxt=nxt)

    own = (kv_f[:, :kvw], kv_f[:, kvw:], ckv_f, kpe_f[:, :rd])
    return x2, h_next, own


def kernel(x_prompt, x_sample, cache_k, cache_v, cache_ckv, cache_kpe, c, c_ctx,
           w_ada, b_ada, w_in, sgu_ln_g, sgu_ln_b, w_s, b_s, q_norm_g, k_norm_g,
           mla_q_norm_g, mla_kv_norm_g, w_uq, w_ukv, w_pa, w_pb, w_pc, w_o,
           ln1_g, ln1_b, ln2_g, ln2_b, w_gate, w_up, w_down):
    Bp, Tp, D = x_prompt.shape
    Bs, Ts, _ = x_sample.shape
    L = w_ada.shape[0]
    A = sgu_ln_g.shape[1]
    hd = q_norm_g.shape[1]
    kvh = cache_k.shape[3]
    qw = w_pb.shape[1]
    kvw = kvh * hd
    qr = mla_q_norm_g.shape[1]
    rank = mla_kv_norm_g.shape[1]
    rd = cache_kpe.shape[-1]
    vd = LANES
    mh = w_pc.shape[1] // vd
    nope = w_uq.shape[2] // mh - rd
    assert hd == LANES and nope == LANES and w_ukv.shape[2] == mh * (nope + vd)
    dims = (A, qw, kvw, qr, rank, rd, D, mh, nope, vd)
    alpha = float((2 * L) ** 0.25)

    R = -(-(1 + Bs) // 8) * 8
    cond = jnp.concatenate([c_ctx[None, :], c, jnp.zeros((R - 1 - Bs, D), F32)], axis=0)
    mods_all = _ada(cond, w_ada, b_ada)

    rope = (_rope_tables(Ts, hd), _rope_tables(Ts, rd))

    xp = x_prompt.reshape(Bp * Tp, D)
    xs = x_sample.reshape(Bs * Ts, D)
    mods = [(mods_all[l, 0:1].reshape(1, 1, 6 * D), mods_all[l, 1:1 + Bs].reshape(Bs, 1, 6 * D))
            for l in range(L)]
    hp = _modulate(xp, mods[0][0], 1, 0, Bp * Tp)
    hs = _modulate(xs, mods[0][1], 1, 0, Ts)

    sw = _stacked_weights(dims, w_in, w_s, b_s, w_uq, w_ukv, w_pa, w_pb, w_pc, w_o, w_gate, w_up, w_down)
    new_k, new_v, new_ckv, new_kpe = [], [], [], []
    for l in range(L):
        w = _layer_weights(l, dims, sw)
        vecs = (sgu_ln_g[l], sgu_ln_b[l], q_norm_g[l], k_norm_g[l], mla_q_norm_g[l], mla_kv_norm_g[l],
                ln1_g[l], ln1_b[l], ln2_g[l], ln2_b[l])
        nxt_p = (mods[l + 1][0], 1, 0) if l + 1 < L else None
        nxt_s = (mods[l + 1][1], 1, 0) if l + 1 < L else None
        xp, hp, own = _trunk_layer(xp, hp, mods[l][0], w, vecs, dims, Bp, Tp, None, None, alpha, nxt_p)
        new_k.append(own[0].reshape(Bp, Tp, kvh, hd))
        new_v.append(own[1].reshape(Bp, Tp, kvh, hd))
        new_ckv.append(own[2].reshape(Bp, Tp, rank))
        new_kpe.append(own[3].reshape(Bp, Tp, rd))
        ctx = (cache_k[:, l], cache_v[:, l], cache_ckv[:, l], cache_kpe[:, l])
        xs, hs, _ = _trunk_layer(xs, hs, mods[l][1], w, vecs, dims, Bs, Ts, rope, ctx, alpha, nxt_s)

    return (xp.reshape(Bp, Tp, D), xs.reshape(Bs, Ts, D),
            jnp.stack(new_k, axis=1), jnp.stack(new_v, axis=1),
            jnp.stack(new_ckv, axis=1), jnp.stack(new_kpe, axis=1))
```

```python
import functools
from typing import NamedTuple

import jax
import jax.numpy as jnp
from jax import lax
from jax.experimental import pallas as pl
from jax.experimental.pallas import tpu as pltpu

F32 = jnp.float32
BF16 = jnp.bfloat16

NORM_EPS = 1e-6
ROPE_THETA = 10000.0
GRID_W = 64

LANES = 128
VMEM_LIMIT_BYTES = 56 * 1024 * 1024


def _cparams(n_axes):
    return pltpu.CompilerParams(
        dimension_semantics=("arbitrary",) * n_axes,
        vmem_limit_bytes=VMEM_LIMIT_BYTES,
    )


def _blk(n, pref):
    b = min(n, pref)
    while n % b:
        b //= 2
    return b


def _rms(a, g):
    ms = jnp.mean(a * a, axis=-1, keepdims=True)
    return a * lax.rsqrt(ms + NORM_EPS) * g


def _rope(y, cos, sin_signed, quarter):
    n = y.shape[-1]
    lane = lax.broadcasted_iota(jnp.int32, y.shape, 1)
    first = (lane & quarter) == 0
    rot = jnp.where(first, pltpu.roll(y, n - quarter, axis=1), pltpu.roll(y, quarter, axis=1))
    return y * cos + rot * sin_signed


def _sigmoid(x):
    return 0.5 * jnp.tanh(0.5 * x) + 0.5


def _layer_norm(z, g, b):
    mu = jnp.mean(z, axis=-1, keepdims=True)
    zc = z - mu
    var = jnp.mean(zc * zc, axis=-1, keepdims=True)
    return zc * lax.rsqrt(var + NORM_EPS) * g + b


def _ada_kernel(c_ref, w_ref, b_ref, o_ref):
    c = c_ref[...]
    s = (c * _sigmoid(c)).astype(BF16)
    o_ref[...] = jnp.dot(s, w_ref[...].astype(BF16), preferred_element_type=F32) + b_ref[...]


def _ada(cond, w_ada, b_ada):
    L, D, N = w_ada.shape
    R = cond.shape[0]
    bn = _blk(N, 512)
    return pl.pallas_call(
        _ada_kernel,
        grid=(L, N // bn),
        in_specs=[
            pl.BlockSpec((R, D), lambda l, j: (0, 0)),
            pl.BlockSpec((None, D, bn), lambda l, j: (l, 0, j)),
            pl.BlockSpec((None, 1, bn), lambda l, j: (l, 0, j)),
        ],
        out_specs=pl.BlockSpec((None, R, bn), lambda l, j: (l, 0, j)),
        out_shape=jax.ShapeDtypeStruct((L, R, N), F32),
        compiler_params=_cparams(2),
        name="ada_mod",
    )(cond, w_ada, b_ada.reshape(L, 1, N))


def _modulate_kernel(x_ref, sc_ref, sh_ref, h_ref):
    h_ref[...] = (x_ref[...] * (1.0 + sc_ref[...]) + sh_ref[...]).astype(h_ref.dtype)


def _mod_spec(D, k, rows_per_group, bt):
    return pl.BlockSpec((None, 1, D), lambda i, *_: ((i * bt) // rows_per_group, 0, k))


def _modulate(x, mods, k_sc, k_sh, rows_per_group):
    M, D = x.shape
    bt = _blk(rows_per_group, 512)
    return pl.pallas_call(
        _modulate_kernel,
        grid=(M // bt,),
        in_specs=[
            pl.BlockSpec((bt, D), lambda i: (i, 0)),
            _mod_spec(D, k_sc, rows_per_group, bt),
            _mod_spec(D, k_sh, rows_per_group, bt),
        ],
        out_specs=pl.BlockSpec((bt, D), lambda i: (i, 0)),
        out_shape=jax.ShapeDtypeStruct((M, D), BF16),
        compiler_params=_cparams(1),
        name="modulate",
    )(x, mods, mods)


def _ln_kernel(x_ref, y_ref, gate_ref, g_ref, b_ref, *rest, alpha, with_h):
    z = alpha * x_ref[...] + gate_ref[...] * y_ref[...].astype(F32)
    xn = _layer_norm(z, g_ref[...], b_ref[...])
    if with_h:
        sc_ref, sh_ref, xo_ref, h_ref = rest
        xo_ref[...] = xn
        h_ref[...] = (xn * (1.0 + sc_ref[...]) + sh_ref[...]).astype(h_ref.dtype)
    else:
        (xo_ref,) = rest
        xo_ref[...] = xn


def _ln_residual(x, y, mods, k_gate, ln_g, ln_b, alpha, rows_per_group, nxt=None):
    M, D = x.shape
    bt = _blk(rows_per_group, 256)
    row = pl.BlockSpec((bt, D), lambda i: (i, 0))
    vec = pl.BlockSpec((1, D), lambda i: (0, 0))
    in_specs = [row, row, _mod_spec(D, k_gate, rows_per_group, bt), vec, vec]
    args = [x, y, mods, ln_g.reshape(1, D), ln_b.reshape(1, D)]
    out_shape = [jax.ShapeDtypeStruct((M, D), F32)]
    out_specs = [row]
    if nxt is not None:
        mods_n, k_sc, k_sh = nxt
        in_specs += [_mod_spec(D, k_sc, rows_per_group, bt), _mod_spec(D, k_sh, rows_per_group, bt)]
        args += [mods_n, mods_n]
        out_shape.append(jax.ShapeDtypeStruct((M, D), BF16))
        out_specs.append(row)
    res = pl.pallas_call(
        functools.partial(_ln_kernel, alpha=alpha, with_h=nxt is not None),
        grid=(M // bt,),
        in_specs=in_specs,
        out_specs=out_specs,
        out_shape=out_shape,
        compiler_params=_cparams(1),
        name="ln_residual",
    )(*args)
    return res if nxt is not None else (res[0], None)


class _W(NamedTuple):
    arr: jax.Array
    layer: int
    col0: int
    n_cols: int


def _wfull(a):
    return _W(a[None], 0, 0, a.shape[1])


def _wspec(w, bn):
    K = w.arr.shape[1]
    l, ob = w.layer, w.col0 // bn
    assert w.col0 % bn == 0 and w.n_cols % bn == 0
    return pl.BlockSpec((None, K, bn), lambda i, j: (l, 0, ob + j))


def _mm_call(kernel, x, ws, bm, bn, extras, extra_specs, out_dtypes, name):
    M, K = x.shape
    n_cols = ws[0].n_cols
    in_specs = [pl.BlockSpec((bm, K), lambda i, j: (i, 0))]
    in_specs += [_wspec(w, bn) for w in ws]
    in_specs += list(extra_specs)
    out_shape = [jax.ShapeDtypeStruct((M, n_cols), dt) for dt in out_dtypes]
    out_specs = [pl.BlockSpec((bm, bn), lambda i, j: (i, j)) for _ in out_dtypes]
    return pl.pallas_call(
        kernel,
        grid=(M // bm, n_cols // bn),
        in_specs=in_specs,
        out_specs=out_specs,
        out_shape=out_shape,
        compiler_params=_cparams(2),
        name=name,
    )(x, *[w.arr for w in ws], *extras)


def _xw(x_ref, w_ref):
    return jnp.dot(x_ref[...], w_ref[...].astype(BF16), preferred_element_type=F32)


def _mm_plain_kernel(x_ref, w_ref, o_ref):
    o_ref[...] = _xw(x_ref, w_ref).astype(o_ref.dtype)


def _mm_plain(x, w, out_dtype, name, bm=1024, bn=1024):
    bm, bn = _blk(x.shape[0], bm), _blk(w.n_cols, bn)
    return _mm_call(_mm_plain_kernel, x, [w], bm, bn, [], [], [out_dtype], name)[0]


def _rope_specs(rope, bm, rows_per_batch):
    if rope is None:
        return [], []
    nb = rows_per_batch // bm
    spec = pl.BlockSpec((bm, LANES), lambda i, j: (i % nb, 0))
    return [rope[0], rope[1]], [spec, spec]


def _mm_heads_kernel(x_ref, w_ref, g_ref, *rest, n_norm, rope, quarter):
    if rope:
        cos_ref, sin_ref, *outs = rest
    else:
        outs = rest
    acc = _xw(x_ref, w_ref)
    for h in range(acc.shape[1] // LANES):
        a = acc[:, h * LANES:(h + 1) * LANES]
        if h < n_norm:
            a = _rms(a, g_ref[...])
            if rope:
                a = _rope(a, cos_ref[...], sin_ref[...], quarter)
        for o in outs:
            o[:, h * LANES:(h + 1) * LANES] = a.astype(o.dtype)


def _mm_heads(x, w, gain, rope, rows_per_batch, n_norm, out_dtypes, name, bm, bn):
    bm, bn = _blk(x.shape[0] if rope is None else rows_per_batch, bm), _blk(w.n_cols, bn)
    rargs, rspecs = _rope_specs(rope, bm, rows_per_batch)
    kern = functools.partial(_mm_heads_kernel, n_norm=n_norm, rope=rope is not None,
                             quarter=LANES // 4)
    return _mm_call(kern, x, [w], bm, bn,
                    [gain.reshape(1, LANES)] + rargs,
                    [pl.BlockSpec((1, LANES), lambda i, j: (0, 0))] + rspecs,
                    out_dtypes, name)


def _mm_rms_kernel(x_ref, w_ref, g_ref, o_ref):
    o_ref[...] = _rms(_xw(x_ref, w_ref), g_ref[...]).astype(o_ref.dtype)


def _mm_rms(x, w, gain, out_dtype, name, bm=1024):
    N = w.n_cols
    bm = _blk(x.shape[0], bm)
    return _mm_call(_mm_rms_kernel, x, [w], bm, N, [gain.reshape(1, N)],
                    [pl.BlockSpec((1, N), lambda i, j: (0, 0))], [out_dtype], name)[0]


def _mm_ckv_kernel(x_ref, w_ref, g_ref, *rest, rank, rope_dim, rope):
    if rope:
        cos_ref, sin_ref, ckv_f, ckv_b, kpe_f, kpe_b = rest
    else:
        ckv_f, ckv_b, kpe_f, kpe_b = rest
    acc = jnp.dot(x_ref[...], w_ref[:, :rank + LANES].astype(BF16), preferred_element_type=F32)
    ckv = _rms(acc[:, :rank], g_ref[...])
    kpe = acc[:, rank:rank + LANES]
    lane = lax.broadcasted_iota(jnp.int32, kpe.shape, 1)
    kpe = jnp.where(lane < rope_dim, kpe, 0.0)
    if rope:
        kpe = _rope(kpe, cos_ref[...], sin_ref[...], rope_dim // 4)
    ckv_f[...] = ckv
    ckv_b[...] = ckv.astype(BF16)
    kpe_f[...] = kpe
    kpe_b[...] = kpe.astype(BF16)


def _mm_ckv(x, w, gain, rope, rows_per_batch, rank, rope_dim, name, bm=1024):
    M, K = x.shape
    bw = w.n_cols
    assert bw >= rank + LANES and w.col0 % bw == 0
    bm = _blk(M if rope is None else rows_per_batch, bm)
    rargs, rspecs = _rope_specs(rope, bm, rows_per_batch)
    kern = functools.partial(_mm_ckv_kernel, rank=rank, rope_dim=rope_dim, rope=rope is not None)
    in_specs = [pl.BlockSpec((bm, K), lambda i, j: (i, 0)),
                _wspec(w, bw),
                pl.BlockSpec((1, rank), lambda i, j: (0, 0))] + rspecs
    out_shape = [jax.ShapeDtypeStruct((M, rank), F32), jax.ShapeDtypeStruct((M, rank), BF16),
                 jax.ShapeDtypeStruct((M, LANES), F32), jax.ShapeDtypeStruct((M, LANES), BF16)]
    out_specs = [pl.BlockSpec((bm, rank), lambda i, j: (i, 0)), pl.BlockSpec((bm, rank), lambda i, j: (i, 0)),
                 pl.BlockSpec((bm, LANES), lambda i, j: (i, 0)), pl.BlockSpec((bm, LANES), lambda i, j: (i, 0))]
    return pl.pallas_call(
        kern, grid=(M // bm, 1), in_specs=in_specs, out_specs=out_specs, out_shape=out_shape,
        compiler_params=_cparams(2), name=name,
    )(x, w.arr, gain.reshape(1, rank), *rargs)


def _mm_uq_kernel(x_ref, w_ref, *rest, rope, quarter):
    if rope:
        cos_ref, sin_ref, o_ref = rest
    else:
        (o_ref,) = rest
    acc = _xw(x_ref, w_ref)
    for h in range(acc.shape[1] // LANES):
        a = acc[:, h * LANES:(h + 1) * LANES]
        if rope and h % 2 == 1:
            a = _rope(a, cos_ref[...], sin_ref[...], quarter)
        o_ref[:, h * LANES:(h + 1) * LANES] = a.astype(o_ref.dtype)


def _mm_uq(x, w, rope, rows_per_batch, rope_dim, name, bm=1024, bn=1024):
    bm, bn = _blk(x.shape[0] if rope is None else rows_per_batch, bm), _blk(w.n_cols, bn)
    rargs, rspecs = _rope_specs(rope, bm, rows_per_batch)
    kern = functools.partial(_mm_uq_kernel, rope=rope is not None, quarter=rope_dim // 4)
    return _mm_call(kern, x, [w], bm, bn, rargs, rspecs, [BF16], name)[0]


def _mm_sigmoid_kernel(x_ref, w_ref, o_ref):
    o_ref[...] = _sigmoid(_xw(x_ref, w_ref)).astype(o_ref.dtype)


def _mm_sigmoid(x, w, out_dtype, name, bm=1024, bn=1024):
    bm, bn = _blk(x.shape[0], bm), _blk(w.n_cols, bn)
    return _mm_call(_mm_sigmoid_kernel, x, [w], bm, bn, [], [], [out_dtype], name)[0]


def _mm_swiglu_kernel(x_ref, wg_ref, wu_ref, o_ref):
    a = _xw(x_ref, wg_ref)
    b = _xw(x_ref, wu_ref)
    o_ref[...] = (a * _sigmoid(a) * b).astype(o_ref.dtype)


def _mm_swiglu(x, wg, wu, name, bm=1024, bn=256):
    bm, bn = _blk(x.shape[0], bm), _blk(wg.n_cols, bn)
    return _mm_call(_mm_swiglu_kernel, x, [wg, wu], bm, bn, [], [], [BF16], name)[0]


def _mm_merge_kernel(ya_ref, yb_ref, yc_ref, wa_ref, wb_ref, wc_ref, ga_ref, gb_ref, gc_ref, o_ref):
    a = jnp.dot(ya_ref[...], wa_ref[...], preferred_element_type=F32)
    b = jnp.dot(yb_ref[...], wb_ref[...], preferred_element_type=F32)
    c = jnp.dot(yc_ref[...], wc_ref[...], preferred_element_type=F32)
    ga, gb, gc = (g[...].astype(F32) for g in (ga_ref, gb_ref, gc_ref))
    o_ref[...] = (ga * a + gb * b + gc * c).astype(o_ref.dtype)


def _mm_merge(ya, yb, yc, wa, wb, wc, gates, name, bm=1024, bn=512):
    M = ya.shape[0]
    D = wa.n_cols
    bm, bn = _blk(M, bm), _blk(D, bn)
    nb = D // bn
    xs = lambda y: pl.BlockSpec((bm, y.shape[1]), lambda i, j: (i, 0))
    gsp = lambda t: pl.BlockSpec((bm, bn), lambda i, j: (i, j + t * nb))
    return pl.pallas_call(
        _mm_merge_kernel,
        grid=(M // bm, nb),
        in_specs=[xs(ya), xs(yb), xs(yc), _wspec(wa, bn), _wspec(wb, bn), _wspec(wc, bn),
                  gsp(0), gsp(1), gsp(2)],
        out_specs=pl.BlockSpec((bm, bn), lambda i, j: (i, j)),
        out_shape=jax.ShapeDtypeStruct((M, D), BF16),
        compiler_params=_cparams(2),
        name=name,
    )(ya, yb, yc, wa.arr, wb.arr, wc.arr, gates, gates, gates)


def _sgu_kernel(u_ref, v_ref, lg_ref, lb_ref, ws_ref, bias_ref, o_ref, *, chunk, groups):
    vn = _layer_norm(v_ref[...], lg_ref[...], lb_ref[...]).astype(BF16)
    gd = vn.shape[1] // groups
    for c in range(vn.shape[0] // chunk):
        r0 = c * chunk
        for g in range(groups):
            c0 = g * gd
            s = jnp.dot(ws_ref[g], vn[r0:r0 + chunk, c0:c0 + gd], preferred_element_type=F32)
            s = s + bias_ref[:, c0:c0 + gd]
            o_ref[r0:r0 + chunk, c0:c0 + gd] = (u_ref[r0:r0 + chunk, c0:c0 + gd] * s).astype(o_ref.dtype)


def _sgu(uv, ln_g, ln_b, w_s, bias_full, name):
    M = uv.shape[0]
    A = uv.shape[1] // 2
    G, C, _ = w_s.shape
    bt = _blk(M, 2 * C)
    return pl.pallas_call(
        functools.partial(_sgu_kernel, chunk=C, groups=G),
        grid=(M // bt,),
        in_specs=[
            pl.BlockSpec((bt, A), lambda i: (i, 0)),
            pl.BlockSpec((bt, A), lambda i: (i, 1)),
            pl.BlockSpec((1, A), lambda i: (0, 0)),
            pl.BlockSpec((1, A), lambda i: (0, 0)),
            pl.BlockSpec((G, C, C), lambda i: (0, 0, 0)),
            pl.BlockSpec((C, A), lambda i: (0, 0)),
        ],
        out_specs=pl.BlockSpec((bt, A), lambda i: (i, 0)),
        out_shape=jax.ShapeDtypeStruct((M, A), BF16),
        compiler_params=_cparams(1),
        name=name,
    )(uv, uv, ln_g.reshape(1, A), ln_b.reshape(1, A), w_s, bias_full)


_NT = (((1,), (1,)), ((), ()))
_LOG2E = 1.4426950408889634


def _attend(q, k, v, scale):
    s = lax.dot_general(q, k, _NT, preferred_element_type=F32)
    m = jnp.max(s, axis=-1, keepdims=True)
    p = jnp.exp2((s - m) * (scale * _LOG2E))
    l = jnp.sum(p, axis=-1, keepdims=True)
    return jnp.dot(p.astype(BF16), v, preferred_element_type=F32) / l


def _chunks(n_heads, rows, sub):
    out = [(h, r, sub) for h in range(n_heads) for r in range(0, rows, sub)]
    if sub >= 512 and len(out) >= 4:
        h, r, _ = out[0]
        out[0:1] = [(h, r, sub // 4), (h, r + sub // 4, sub - sub // 4)]
        h, r, _ = out[-1]
        out[-1:] = [(h, r, sub - sub // 4), (h, r + sub - sub // 4, sub // 4)]
    return out


def _gqa_kernel(q_ref, g_ref, *rest, nkv, group, hd, scale, sub, rope):
    if rope:
        cos_ref, sin_ref, k_ref, v_ref, o_ref = rest
    else:
        k_ref, v_ref, o_ref = rest
    for h, r, n in _chunks(nkv * group, q_ref.shape[0], sub):
        kv = h // group
        q = _rms(q_ref[r:r + n, h * hd:(h + 1) * hd], g_ref[...])
        if rope:
            q = _rope(q, cos_ref[r:r + n, :], sin_ref[r:r + n, :], hd // 4)
        o = _attend(q.astype(BF16), k_ref[:, kv * hd:(kv + 1) * hd],
                    v_ref[:, kv * hd:(kv + 1) * hd], scale)
        o_ref[r:r + n, h * hd:(h + 1) * hd] = o.astype(o_ref.dtype)


def _gqa(q, gain, rope, k, v, kv_heads, hd, name, bq, nkv, sub=512):
    B, Nq, W = q.shape
    S = k.shape[1]
    group = W // (kv_heads * hd)
    bq = _blk(Nq, bq)
    gw = nkv * group * hd
    sub = _blk(bq, sub)
    tab = pl.BlockSpec((bq, hd), lambda b, n, i: (i, 0))
    return pl.pallas_call(
        functools.partial(_gqa_kernel, nkv=nkv, group=group, hd=hd, scale=hd ** -0.5, sub=sub,
                          rope=rope is not None),
        grid=(B, kv_heads // nkv, Nq // bq),
        in_specs=[
            pl.BlockSpec((None, bq, gw), lambda b, n, i: (b, i, n)),
            pl.BlockSpec((1, hd), lambda b, n, i: (0, 0)),
        ] + ([tab, tab] if rope is not None else []) + [
            pl.BlockSpec((None, S, nkv * hd), lambda b, n, i: (b, 0, n)),
            pl.BlockSpec((None, S, nkv * hd), lambda b, n, i: (b, 0, n)),
        ],
        out_specs=pl.BlockSpec((None, bq, gw), lambda b, n, i: (b, i, n)),
        out_shape=jax.ShapeDtypeStruct((B, Nq, W), BF16),
        compiler_params=_cparams(3),
        name=name,
    )(q, gain.reshape(1, hd), *(rope if rope is not None else ()), k, v)


def _mla_kernel(q_ref, kn_ref, kpe_ref, v_ref, o_ref, *, hb, scale, sub):
    kpe = kpe_ref[...]
    ks = [jnp.concatenate([kn_ref[:, h * LANES:(h + 1) * LANES], kpe], axis=1) for h in range(hb)]
    for h, r, n in _chunks(hb, q_ref.shape[0], sub):
        o = _attend(q_ref[r:r + n, 2 * h * LANES:2 * (h + 1) * LANES], ks[h],
                    v_ref[:, h * LANES:(h + 1) * LANES], scale)
        o_ref[r:r + n, h * LANES:(h + 1) * LANES] = o.astype(o_ref.dtype)


def _mla(q, kv, kpe, heads, scale, name, bq, hb, sub=512):
    B, Nq, _ = q.shape
    S = kv.shape[1]
    bq = _blk(Nq, bq)
    sub = _blk(bq, sub)
    nhb = heads // hb
    return pl.pallas_call(
        functools.partial(_mla_kernel, hb=hb, scale=scale, sub=sub),
        grid=(B, nhb, Nq // bq),
        in_specs=[
            pl.BlockSpec((None, bq, 2 * hb * LANES), lambda b, h, i: (b, i, h)),
            pl.BlockSpec((None, S, hb * LANES), lambda b, h, i: (b, 0, h)),
            pl.BlockSpec((None, S, LANES), lambda b, h, i: (b, 0, 0)),
            pl.BlockSpec((None, S, hb * LANES), lambda b, h, i: (b, 0, nhb + h)),
        ],
        out_specs=pl.BlockSpec((None, bq, hb * LANES), lambda b, h, i: (b, i, h)),
        out_shape=jax.ShapeDtypeStruct((B, Nq, heads * LANES), BF16),
        compiler_params=_cparams(3),
        name=name,
    )(q, kv, kpe, kv)


def _rope_tables(n, dim):
    rows = n // GRID_W
    row = jnp.repeat(jnp.arange(rows, dtype=F32), GRID_W)
    col = jnp.tile(jnp.arange(GRID_W, dtype=F32), rows)
    quarter = dim // 4
    freqs = ROPE_THETA ** (-jnp.arange(quarter, dtype=F32) / quarter)
    ra = row[:, None] * freqs[None, :]
    ca = col[:, None] * freqs[None, :]
    ang = jnp.concatenate([ra, ra, ca, ca], axis=-1)
    sign = jnp.where((jnp.arange(dim) // quarter) % 2 == 0, -1.0, 1.0).astype(F32)
    cos = jnp.cos(ang)
    sin = jnp.sin(ang) * sign[None, :]
    if dim < LANES:
        cos = jnp.pad(cos, ((0, 0), (0, LANES - dim)), constant_values=1.0)
        sin = jnp.pad(sin, ((0, 0), (0, LANES - dim)))
    return cos, sin


def _stacked_weights(dims, w_in, w_s, b_s, w_uq, w_ukv, w_pa, w_pb, w_pc, w_o, w_gate, w_up, w_down):
    A, qw, kvw, qr, rank, rd, D, mh, nope, vd = dims
    L = w_in.shape[0]
    bf = lambda a: a.astype(BF16)
    o_c = 2 * A + qw + 2 * kvw + qr
    o_g = o_c + rank + rd
    s = {}
    s['in'] = bf(w_in[..., :o_c])
    s['ckv'] = bf(jnp.pad(w_in[..., o_c:o_g], ((0, 0), (0, 0), (0, LANES - rd))))
    s['g'] = bf(w_in[..., o_g:])
    uq = w_uq.reshape(L, qr, mh, nope + rd)
    s['uq'] = bf(jnp.pad(uq, ((0, 0), (0, 0), (0, 0), (0, 2 * LANES - nope - rd))).reshape(L, qr, mh * 2 * LANES))
    ukv = w_ukv.reshape(L, rank, mh, nope + vd)
    s['ukv'] = bf(jnp.concatenate([ukv[..., :nope].reshape(L, rank, mh * nope),
                                   ukv[..., nope:].reshape(L, rank, mh * vd)], axis=-1))
    for name, a in (('pa', w_pa), ('pb', w_pb), ('pc', w_pc), ('o', w_o), ('down', w_down), ('s', w_s)):
        s[name] = bf(a)
    s['gate'], s['up'] = w_gate, w_up
    gd = A // w_s.shape[1]
    s['sb'] = jnp.repeat(jnp.swapaxes(b_s, 1, 2), gd, axis=2)
    return s


def _layer_weights(l, dims, s):
    A, qw, kvw, qr, rank, rd, D, mh, nope, vd = dims
    o = [0, 2 * A, 2 * A + qw, 2 * A + qw + 2 * kvw, 2 * A + qw + 2 * kvw + qr]
    full = lambda a: _W(a, l, 0, a.shape[2])
    w = {}
    w['uv'] = _W(s['in'], l, o[0], 2 * A)
    w['q'] = _W(s['in'], l, o[1], qw)
    w['kv'] = _W(s['in'], l, o[2], 2 * kvw)
    w['cq'] = _W(s['in'], l, o[3], qr)
    for name in ('ckv', 'g', 'uq', 'ukv', 'pa', 'pb', 'pc', 'o', 'gate', 'up', 'down'):
        w[name] = full(s[name])
    w['s'] = s['s'][l]
    w['sb'] = s['sb'][l]
    return w


def _trunk_layer(x, h, mods, w, vecs, dims, B, T, rope, ctx, alpha, nxt):
    A, qw, kvw, qr, rank, rd, D, mh, nope, vd = dims
    sgu_g, sgu_b, qg, kg, cqg, ckvg, ln1g, ln1b, ln2g, ln2b = vecs
    M = B * T
    G = mods.shape[0]
    rpg = M // G
    hd = LANES
    kvh = kvw // hd
    tag = "s" if rope is not None else "p"
    rope_g = None if rope is None else rope[0]
    rope_m = None if rope is None else rope[1]

    uv = _mm_plain(h, w['uv'], F32, "proj_uv_" + tag)
    q = _mm_plain(h, w['q'], F32, "proj_q_" + tag)
    kv_f, kv_b = _mm_heads(h, w['kv'], kg, rope_g, T, kvh, [F32, BF16], "proj_kv_" + tag,
                           bm=1024, bn=2 * kvw)
    cq = _mm_rms(h, w['cq'], cqg, BF16, "proj_cq_" + tag)
    ckv_f, ckv_b, kpe_f, kpe_b = _mm_ckv(h, w['ckv'], ckvg, rope_m, T, rank, rd, "proj_ckv_" + tag)
    gates = _mm_sigmoid(h, w['g'], BF16, "proj_gates_" + tag)

    y_a = _sgu(uv, sgu_g, sgu_b, w['s'], w['sb'], "sgu_" + tag)

    k_b = kv_b[:, :kvw].reshape(B, T, kvw)
    v_b = kv_b[:, kvw:].reshape(B, T, kvw)
    if ctx is not None:
        c_k, c_v, c_ckv, c_kpe = ctx
        P = c_k.shape[1]
        k_b = jnp.concatenate([c_k.reshape(B, P, kvw).astype(BF16), k_b], axis=1)
        v_b = jnp.concatenate([c_v.reshape(B, P, kvw).astype(BF16), v_b], axis=1)
    long_keys = ctx is not None
    y_b = _gqa(q.reshape(B, T, qw), qg, rope_g, k_b, v_b, kvh, hd, "gqa_" + tag,
               bq=1024 if long_keys else 256, nkv=1 if long_keys else kvh).reshape(M, qw)

    qc = _mm_uq(cq, w['uq'], rope_m, T, rd, "mla_uq_" + tag)
    ckv_all = ckv_b.reshape(B, T, rank)
    kpe_all = kpe_b.reshape(B, T, LANES)
    if ctx is not None:
        ckv_all = jnp.concatenate([c_ckv.astype(BF16), ckv_all], axis=1)
        c_kpe_pad = jnp.pad(c_kpe, ((0, 0), (0, 0), (0, LANES - rd))).astype(BF16)
        kpe_all = jnp.concatenate([c_kpe_pad, kpe_all], axis=1)
    S = ckv_all.shape[1]
    kvu = _mm_plain(ckv_all.reshape(B * S, rank), w['ukv'], BF16, "mla_ukv_" + tag)
    y_c = _mla(qc.reshape(B, T, mh * 2 * LANES), kvu.reshape(B, S, mh * (nope + vd)), kpe_all,
               mh, (nope + rd) ** -0.5, "mla_" + tag,
               bq=4096 if long_keys else 256, hb=1 if long_keys else mh).reshape(M, mh * vd)

    merged = _mm_merge(y_a, y_b, y_c, w['pa'], w['pb'], w['pc'], gates, "merge_" + tag)
    mix = _mm_plain(merged, w['o'], BF16, "proj_o_" + tag)
    x1, h2 = _ln_residual(x, mix, mods, 2, ln1g, ln1b, alpha, rpg, nxt=(mods, 4, 3))
    hid = _mm_swiglu(h2, w['gate'], w['up'], "ffn_in_" + tag)
    ff = _mm_plain(hid, w['down'], BF16, "ffn_out_" + tag, bm=512, bn=512)
    x2, h_next = _ln_residual(x1, ff, mods, 5, ln2g, ln2b, alpha, rpg, nxt=nxt)

    own = (kv_f[:, :kvw], kv_f[:, kvw:], ckv_f, kpe_f[:, :rd])
    return x2, h_next, own


def kernel(x_prompt, x_sample, cache_k, cache_v, cache_ckv, cache_kpe, c, c_ctx,
           w_ada, b_ada, w_in, sgu_ln_g, sgu_ln_b, w_s, b_s, q_norm_g, k_norm_g,
           mla_q_norm_g, mla_kv_norm_g, w_uq, w_ukv, w_pa, w_pb, w_pc, w_o,
           ln1_g, ln1_b, ln2_g, ln2_b, w_gate, w_up, w_down):
    Bp, Tp, D = x_prompt.shape
    Bs, Ts, _ = x_sample.shape
    L = w_ada.shape[0]
    A = sgu_ln_g.shape[1]
    hd = q_norm_g.shape[1]
    kvh = cache_k.shape[3]
    qw = w_pb.shape[1]
    kvw = kvh * hd
    qr = mla_q_norm_g.shape[1]
    rank = mla_kv_norm_g.shape[1]
    rd = cache_kpe.shape[-1]
    vd = LANES
    mh = w_pc.shape[1] // vd
    nope = w_uq.shape[2] // mh - rd
    assert hd == LANES and nope == LANES and w_ukv.shape[2] == mh * (nope + vd)
    dims = (A, qw, kvw, qr, rank, rd, D, mh, nope, vd)
    alpha = float((2 * L) ** 0.25)

    R = -(-(1 + Bs) // 8) * 8
    cond = jnp.concatenate([c_ctx[None, :], c, jnp.zeros((R - 1 - Bs, D), F32)], axis=0)
    mods_all = _ada(cond, w_ada, b_ada)

    rope = (_rope_tables(Ts, hd), _rope_tables(Ts, rd))

    xp = x_prompt.reshape(Bp * Tp, D)
    xs = x_sample.reshape(Bs * Ts, D)
    mods = [(mods_all[l, 0:1].reshape(1, 1, 6 * D), mods_all[l, 1:1 + Bs].reshape(Bs, 1, 6 * D))
            for l in range(L)]
    hp = _modulate(xp, mods[0][0], 1, 0, Bp * Tp)
    hs = _modulate(xs, mods[0][1], 1, 0, Ts)

    sw = _stacked_weights(dims, w_in, w_s, b_s, w_uq, w_ukv, w_pa, w_pb, w_pc, w_o, w_gate, w_up, w_down)
    new_k, new_v, new_ckv, new_kpe = [], [], [], []
    for l in range(L):
        w = _layer_weights(l, dims, sw)
        vecs = (sgu_ln_g[l], sgu_ln_b[l], q_norm_g[l], k_norm_g[l], mla_q_norm_g[l], mla_kv_norm_g[l],
                ln1_g[l], ln1_b[l], ln2_g[l], ln2_b[l])
        nxt_p = (mods[l + 1][0], 1, 0) if l + 1 < L else None
        nxt_s = (mods[l + 1][1], 1, 0) if l + 1 < L else None
        xp, hp, own = _trunk_layer(xp, hp, mods[l][0], w, vecs, dims, Bp, Tp, None, None, alpha, nxt_p)
        new_k.append(own[0].reshape(Bp, Tp, kvh, hd))
        new_v.append(own[1].reshape(Bp, Tp, kvh, hd))
        new_ckv.append(own[2].reshape(Bp, Tp, rank))
        new_kpe.append(own[3].reshape(Bp, Tp, rd))
        ctx = (cache_k[:, l], cache_v[:, l], cache_ckv[:, l], cache_kpe[:, l])
        xs, hs, _ = _trunk_layer(xs, hs, mods[l][1], w, vecs, dims, Bs, Ts, rope, ctx, alpha, nxt_s)

    return (xp.reshape(Bp, Tp, D), xs.reshape(Bs, Ts, D),
            jnp.stack(new_k, axis=1), jnp.stack(new_v, axis=1),
            jnp.stack(new_ckv, axis=1), jnp.stack(new_kpe, axis=1))
```

```python
import functools
from typing import NamedTuple

import jax
import jax.numpy as jnp
from jax import lax
from jax.experimental import pallas as pl
from jax.experimental.pallas import tpu as pltpu

F32 = jnp.float32
BF16 = jnp.bfloat16

NORM_EPS = 1e-6
ROPE_THETA = 10000.0
GRID_W = 64

LANES = 128
VMEM_LIMIT_BYTES = 56 * 1024 * 1024


def _cparams(n_axes):
    return pltpu.CompilerParams(
        dimension_semantics=("arbitrary",) * n_axes,
        vmem_limit_bytes=VMEM_LIMIT_BYTES,
    )


def _blk(n, pref):
    b = min(n, pref)
    while n % b:
        b //= 2
    return b


def _rms(a, g):
    ms = jnp.mean(a * a, axis=-1, keepdims=True)
    return a * lax.rsqrt(ms + NORM_EPS) * g


def _rope(y, cos, sin_signed, quarter):
    n = y.shape[-1]
    lane = lax.broadcasted_iota(jnp.int32, y.shape, 1)
    first = (lane & quarter) == 0
    rot = jnp.where(first, pltpu.roll(y, n - quarter, axis=1), pltpu.roll(y, quarter, axis=1))
    return y * cos + rot * sin_signed


def _sigmoid(x):
    return 0.5 * jnp.tanh(0.5 * x) + 0.5


def _layer_norm(z, g, b):
    mu = jnp.mean(z, axis=-1, keepdims=True)
    zc = z - mu
    var = jnp.mean(zc * zc, axis=-1, keepdims=True)
    return zc * lax.rsqrt(var + NORM_EPS) * g + b


def _ada_kernel(c_ref, w_ref, b_ref, o_ref):
    c = c_ref[...]
    s = (c * _sigmoid(c)).astype(BF16)
    o_ref[...] = jnp.dot(s, w_ref[...].astype(BF16), preferred_element_type=F32) + b_ref[...]


def _ada(cond, w_ada, b_ada):
    L, D, N = w_ada.shape
    R = cond.shape[0]
    bn = _blk(N, 512)
    return pl.pallas_call(
        _ada_kernel,
        grid=(L, N // bn),
        in_specs=[
            pl.BlockSpec((R, D), lambda l, j: (0, 0)),
            pl.BlockSpec((None, D, bn), lambda l, j: (l, 0, j)),
            pl.BlockSpec((None, 1, bn), lambda l, j: (l, 0, j)),
        ],
        out_specs=pl.BlockSpec((None, R, bn), lambda l, j: (l, 0, j)),
        out_shape=jax.ShapeDtypeStruct((L, R, N), F32),
        compiler_params=_cparams(2),
        name="ada_mod",
    )(cond, w_ada, b_ada.reshape(L, 1, N))


def _modulate_kernel(x_ref, sc_ref, sh_ref, h_ref):
    h_ref[...] = (x_ref[...] * (1.0 + sc_ref[...]) + sh_ref[...]).astype(h_ref.dtype)


def _mod_spec(D, k, rows_per_group, bt):
    return pl.BlockSpec((None, 1, D), lambda i, *_: ((i * bt) // rows_per_group, 0, k))


def _modulate(x, mods, k_sc, k_sh, rows_per_group):
    M, D = x.shape
    bt = _blk(rows_per_group, 512)
    return pl.pallas_call(
        _modulate_kernel,
        grid=(M // bt,),
        in_specs=[
            pl.BlockSpec((bt, D), lambda i: (i, 0)),
            _mod_spec(D, k_sc, rows_per_group, bt),
            _mod_spec(D, k_sh, rows_per_group, bt),
        ],
        out_specs=pl.BlockSpec((bt, D), lambda i: (i, 0)),
        out_shape=jax.ShapeDtypeStruct((M, D), BF16),
        compiler_params=_cparams(1),
        name="modulate",
    )(x, mods, mods)


def _ln_kernel(x_ref, y_ref, gate_ref, g_ref, b_ref, *rest, alpha, with_h):
    z = alpha * x_ref[...] + gate_ref[...] * y_ref[...].astype(F32)
    xn = _layer_norm(z, g_ref[...], b_ref[...])
    if with_h:
        sc_ref, sh_ref, xo_ref, h_ref = rest
        xo_ref[...] = xn
        h_ref[...] = (xn * (1.0 + sc_ref[...]) + sh_ref[...]).astype(h_ref.dtype)
    else:
        (xo_ref,) = rest
        xo_ref[...] = xn


def _ln_residual(x, y, mods, k_gate, ln_g, ln_b, alpha, rows_per_group, nxt=None):
    M, D = x.shape
    bt = _blk(rows_per_group, 256)
    row = pl.BlockSpec((bt, D), lambda i: (i, 0))
    vec = pl.BlockSpec((1, D), lambda i: (0, 0))
    in_specs = [row, row, _mod_spec(D, k_gate, rows_per_group, bt), vec, vec]
    args = [x, y, mods, ln_g.reshape(1, D), ln_b.reshape(1, D)]
    out_shape = [jax.ShapeDtypeStruct((M, D), F32)]
    out_specs = [row]
    if nxt is not None:
        mods_n, k_sc, k_sh = nxt
        in_specs += [_mod_spec(D, k_sc, rows_per_group, bt), _mod_spec(D, k_sh, rows_per_group, bt)]
        args += [mods_n, mods_n]
        out_shape.append(jax.ShapeDtypeStruct((M, D), BF16))
        out_specs.append(row)
    res = pl.pallas_call(
        functools.partial(_ln_kernel, alpha=alpha, with_h=nxt is not None),
        grid=(M // bt,),
        in_specs=in_specs,
        out_specs=out_specs,
        out_shape=out_shape,
        compiler_params=_cparams(1),
        name="ln_residual",
    )(*args)
    return res if nxt is not None else (res[0], None)


class _W(NamedTuple):
    arr: jax.Array
    layer: int
    col0: int
    n_cols: int


def _wfull(a):
    return _W(a[None], 0, 0, a.shape[1])


def _wspec(w, bn):
    K = w.arr.shape[1]
    l, ob = w.layer, w.col0 // bn
    assert w.col0 % bn == 0 and w.n_cols % bn == 0
    return pl.BlockSpec((None, K, bn), lambda i, j: (l, 0, ob + j))


def _mm_call(kernel, x, ws, bm, bn, extras, extra_specs, out_dtypes, name):
    M, K = x.shape
    n_cols = ws[0].n_cols
    in_specs = [pl.BlockSpec((bm, K), lambda i, j: (i, 0))]
    in_specs += [_wspec(w, bn) for w in ws]
    in_specs += list(extra_specs)
    out_shape = [jax.ShapeDtypeStruct((M, n_cols), dt) for dt in out_dtypes]
    out_specs = [pl.BlockSpec((bm, bn), lambda i, j: (i, j)) for _ in out_dtypes]
    return pl.pallas_call(
        kernel,
        grid=(M // bm, n_cols // bn),
        in_specs=in_specs,
        out_specs=out_specs,
        out_shape=out_shape,
        compiler_params=_cparams(2),
        name=name,
    )(x, *[w.arr for w in ws], *extras)


def _xw(x_ref, w_ref):
    return jnp.dot(x_ref[...], w_ref[...].astype(BF16), preferred_element_type=F32)


def _mm_plain_kernel(x_ref, w_ref, o_ref):
    o_ref[...] = _xw(x_ref, w_ref).astype(o_ref.dtype)


def _mm_plain(x, w, out_dtype, name, bm=1024, bn=1024):
    bm, bn = _blk(x.shape[0], bm), _blk(w.n_cols, bn)
    return _mm_call(_mm_plain_kernel, x, [w], bm, bn, [], [], [out_dtype], name)[0]


def _rope_specs(rope, bm, rows_per_batch):
    if rope is None:
        return [], []
    nb = rows_per_batch // bm
    spec = pl.BlockSpec((bm, LANES), lambda i, j: (i % nb, 0))
    return [rope[0], rope[1]], [spec, spec]


def _mm_heads_kernel(x_ref, w_ref, g_ref, *rest, n_norm, rope, quarter):
    if rope:
        cos_ref, sin_ref, *outs = rest
    else:
        outs = rest
    acc = _xw(x_ref, w_ref)
    for h in range(acc.shape[1] // LANES):
        a = acc[:, h * LANES:(h + 1) * LANES]
        if h < n_norm:
            a = _rms(a, g_ref[...])
            if rope:
                a = _rope(a, cos_ref[...], sin_ref[...], quarter)
        for o in outs:
            o[:, h * LANES:(h + 1) * LANES] = a.astype(o.dtype)


def _mm_heads(x, w, gain, rope, rows_per_batch, n_norm, out_dtypes, name, bm, bn):
    bm, bn = _blk(x.shape[0] if rope is None else rows_per_batch, bm), _blk(w.n_cols, bn)
    rargs, rspecs = _rope_specs(rope, bm, rows_per_batch)
    kern = functools.partial(_mm_heads_kernel, n_norm=n_norm, rope=rope is not None,
                             quarter=LANES // 4)
    return _mm_call(kern, x, [w], bm, bn,
                    [gain.reshape(1, LANES)] + rargs,
                    [pl.BlockSpec((1, LANES), lambda i, j: (0, 0))] + rspecs,
                    out_dtypes, name)


def _mm_rms_kernel(x_ref, w_ref, g_ref, o_ref):
    o_ref[...] = _rms(_xw(x_ref, w_ref), g_ref[...]).astype(o_ref.dtype)


def _mm_rms(x, w, gain, out_dtype, name, bm=1024):
    N = w.n_cols
    bm = _blk(x.shape[0], bm)
    return _mm_call(_mm_rms_kernel, x, [w], bm, N, [gain.reshape(1, N)],
                    [pl.BlockSpec((1, N), lambda i, j: (0, 0))], [out_dtype], name)[0]


def _mm_ckv_kernel(x_ref, w_ref, g_ref, *rest, rank, rope_dim, rope):
    if rope:
        cos_ref, sin_ref, ckv_f, ckv_b, kpe_f, kpe_b = rest
    else:
        ckv_f, ckv_b, kpe_f, kpe_b = rest
    acc = jnp.dot(x_ref[...], w_ref[:, :rank + LANES].astype(BF16), preferred_element_type=F32)
    ckv = _rms(acc[:, :rank], g_ref[...])
    kpe = acc[:, rank:rank + LANES]
    lane = lax.broadcasted_iota(jnp.int32, kpe.shape, 1)
    kpe = jnp.where(lane < rope_dim, kpe, 0.0)
    if rope:
        kpe = _rope(kpe, cos_ref[...], sin_ref[...], rope_dim // 4)
    ckv_f[...] = ckv
    ckv_b[...] = ckv.astype(BF16)
    kpe_f[...] = kpe
    kpe_b[...] = kpe.astype(BF16)


def _mm_ckv(x, w, gain, rope, rows_per_batch, rank, rope_dim, name, bm=1024):
    M, K = x.shape
    bw = w.n_cols
    assert bw >= rank + LANES and w.col0 % bw == 0
    bm = _blk(M if rope is None else rows_per_batch, bm)
    rargs, rspecs = _rope_specs(rope, bm, rows_per_batch)
    kern = functools.partial(_mm_ckv_kernel, rank=rank, rope_dim=rope_dim, rope=rope is not None)
    in_specs = [pl.BlockSpec((bm, K), lambda i, j: (i, 0)),
                _wspec(w, bw),
                pl.BlockSpec((1, rank), lambda i, j: (0, 0))] + rspecs
    out_shape = [jax.ShapeDtypeStruct((M, rank), F32), jax.ShapeDtypeStruct((M, rank), BF16),
                 jax.ShapeDtypeStruct((M, LANES), F32), jax.ShapeDtypeStruct((M, LANES), BF16)]
    out_specs = [pl.BlockSpec((bm, rank), lambda i, j: (i, 0)), pl.BlockSpec((bm, rank), lambda i, j: (i, 0)),
                 pl.BlockSpec((bm, LANES), lambda i, j: (i, 0)), pl.BlockSpec((bm, LANES), lambda i, j: (i, 0))]
    return pl.pallas_call(
        kern, grid=(M // bm, 1), in_specs=in_specs, out_specs=out_specs, out_shape=out_shape,
        compiler_params=_cparams(2), name=name,
    )(x, w.arr, gain.reshape(1, rank), *rargs)


def _mm_uq_kernel(x_ref, w_ref, *rest, rope, quarter):
    if rope:
        cos_ref, sin_ref, o_ref = rest
    else:
        (o_ref,) = rest
    acc = _xw(x_ref, w_ref)
    for h in range(acc.shape[1] // LANES):
        a = acc[:, h * LANES:(h + 1) * LANES]
        if rope and h % 2 == 1:
            a = _rope(a, cos_ref[...], sin_ref[...], quarter)
        o_ref[:, h * LANES:(h + 1) * LANES] = a.astype(o_ref.dtype)


def _mm_uq(x, w, rope, rows_per_batch, rope_dim, name, bm=1024, bn=1024):
    bm, bn = _blk(x.shape[0] if rope is None else rows_per_batch, bm), _blk(w.n_cols, bn)
    rargs, rspecs = _rope_specs(rope, bm, rows_per_batch)
    kern = functools.partial(_mm_uq_kernel, rope=rope is not None, quarter=rope_dim // 4)
    return _mm_call(kern, x, [w], bm, bn, rargs, rspecs, [BF16], name)[0]


def _mm_sigmoid_kernel(x_ref, w_ref, o_ref):
    o_ref[...] = _sigmoid(_xw(x_ref, w_ref)).astype(o_ref.dtype)


def _mm_sigmoid(x, w, out_dtype, name, bm=1024, bn=1024):
    bm, bn = _blk(x.shape[0], bm), _blk(w.n_cols, bn)
    return _mm_call(_mm_sigmoid_kernel, x, [w], bm, bn, [], [], [out_dtype], name)[0]


def _mm_swiglu_kernel(x_ref, wg_ref, wu_ref, o_ref):
    a = _xw(x_ref, wg_ref)
    b = _xw(x_ref, wu_ref)
    o_ref[...] = (a * _sigmoid(a) * b).astype(o_ref.dtype)


def _mm_swiglu(x, wg, wu, name, bm=1024, bn=256):
    bm, bn = _blk(x.shape[0], bm), _blk(wg.n_cols, bn)
    return _mm_call(_mm_swiglu_kernel, x, [wg, wu], bm, bn, [], [], [BF16], name)[0]


def _mm_merge_kernel(ya_ref, yb_ref, yc_ref, wa_ref, wb_ref, wc_ref, ga_ref, gb_ref, gc_ref, o_ref):
    a = jnp.dot(ya_ref[...], wa_ref[...], preferred_element_type=F32)
    b = jnp.dot(yb_ref[...], wb_ref[...], preferred_element_type=F32)
    c = jnp.dot(yc_ref[...], wc_ref[...], preferred_element_type=F32)
    ga, gb, gc = (g[...].astype(F32) for g in (ga_ref, gb_ref, gc_ref))
    o_ref[...] = (ga * a + gb * b + gc * c).astype(o_ref.dtype)


def _mm_merge(ya, yb, yc, wa, wb, wc, gates, name, bm=1024, bn=512):
    M = ya.shape[0]
    D = wa.n_cols
    bm, bn = _blk(M, bm), _blk(D, bn)
    nb = D // bn
    xs = lambda y: pl.BlockSpec((bm, y.shape[1]), lambda i, j: (i, 0))
    gsp = lambda t: pl.BlockSpec((bm, bn), lambda i, j: (i, j + t * nb))
    return pl.pallas_call(
        _mm_merge_kernel,
        grid=(M // bm, nb),
        in_specs=[xs(ya), xs(yb), xs(yc), _wspec(wa, bn), _wspec(wb, bn), _wspec(wc, bn),
                  gsp(0), gsp(1), gsp(2)],
        out_specs=pl.BlockSpec((bm, bn), lambda i, j: (i, j)),
        out_shape=jax.ShapeDtypeStruct((M, D), BF16),
        compiler_params=_cparams(2),
        name=name,
    )(ya, yb, yc, wa.arr, wb.arr, wc.arr, gates, gates, gates)


def _sgu_kernel(u_ref, v_ref, lg_ref, lb_ref, ws_ref, bias_ref, o_ref, *, chunk, groups):
    vn = _layer_norm(v_ref[...], lg_ref[...], lb_ref[...]).astype(BF16)
    gd = vn.shape[1] // groups
    for c in range(vn.shape[0] // chunk):
        r0 = c * chunk
        for g in range(groups):
            c0 = g * gd
            s = jnp.dot(ws_ref[g], vn[r0:r0 + chunk, c0:c0 + gd], preferred_element_type=F32)
            s = s + bias_ref[:, c0:c0 + gd]
            o_ref[r0:r0 + chunk, c0:c0 + gd] = (u_ref[r0:r0 + chunk, c0:c0 + gd] * s).astype(o_ref.dtype)


def _sgu(uv, ln_g, ln_b, w_s, bias_full, name):
    M = uv.shape[0]
    A = uv.shape[1] // 2
    G, C, _ = w_s.shape
    bt = _blk(M, 2 * C)
    return pl.pallas_call(
        functools.partial(_sgu_kernel, chunk=C, groups=G),
        grid=(M // bt,),
        in_specs=[
            pl.BlockSpec((bt, A), lambda i: (i, 0)),
            pl.BlockSpec((bt, A), lambda i: (i, 1)),
            pl.BlockSpec((1, A), lambda i: (0, 0)),
            pl.BlockSpec((1, A), lambda i: (0, 0)),
            pl.BlockSpec((G, C, C), lambda i: (0, 0, 0)),
            pl.BlockSpec((C, A), lambda i: (0, 0)),
        ],
        out_specs=pl.BlockSpec((bt, A), lambda i: (i, 0)),
        out_shape=jax.ShapeDtypeStruct((M, A), BF16),
        compiler_params=_cparams(1),
        name=name,
    )(uv, uv, ln_g.reshape(1, A), ln_b.reshape(1, A), w_s, bias_full)


_NT = (((1,), (1,)), ((), ()))
_LOG2E = 1.4426950408889634


def _attend(q, k, v, scale):
    s = lax.dot_general(q, k, _NT, preferred_element_type=F32)
    m = jnp.max(s, axis=-1, keepdims=True)
    p = jnp.exp2((s - m) * (scale * _LOG2E))
    l = jnp.sum(p, axis=-1, keepdims=True)
    return jnp.dot(p.astype(BF16), v, preferred_element_type=F32) / l


def _chunks(n_heads, rows, sub):
    out = [(h, r, sub) for h in range(n_heads) for r in range(0, rows, sub)]
    if sub >= 512 and len(out) >= 4:
        h, r, _ = out[0]
        out[0:1] = [(h, r, sub // 4), (h, r + sub // 4, sub - sub // 4)]
        h, r, _ = out[-1]
        out[-1:] = [(h, r, sub - sub // 4), (h, r + sub - sub // 4, sub // 4)]
    return out


def _gqa_kernel(q_ref, g_ref, *rest, nkv, group, hd, scale, sub, rope):
    if rope:
        cos_ref, sin_ref, k_ref, v_ref, o_ref = rest
    else:
        k_ref, v_ref, o_ref = rest
    for h, r, n in _chunks(nkv * group, q_ref.shape[0], sub):
        kv = h // group
        q = _rms(q_ref[r:r + n, h * hd:(h + 1) * hd], g_ref[...])
        if rope:
            q = _rope(q, cos_ref[r:r + n, :], sin_ref[r:r + n, :], hd // 4)
        o = _attend(q.astype(BF16), k_ref[:, kv * hd:(kv + 1) * hd],
                    v_ref[:, kv * hd:(kv + 1) * hd], scale)
        o_ref[r:r + n, h * hd:(h + 1) * hd] = o.astype(o_ref.dtype)


def _gqa(q, gain, rope, k, v, kv_heads, hd, name, bq, nkv, sub=512):
    B, Nq, W = q.shape
    S = k.shape[1]
    group = W // (kv_heads * hd)
    bq = _blk(Nq, bq)
    gw = nkv * group * hd
    sub = _blk(bq, sub)
    tab = pl.BlockSpec((bq, hd), lambda b, n, i: (i, 0))
    return pl.pallas_call(
        functools.partial(_gqa_kernel, nkv=nkv, group=group, hd=hd, scale=hd ** -0.5, sub=sub,
                          rope=rope is not None),
        grid=(B, kv_heads // nkv, Nq // bq),
        in_specs=[
            pl.BlockSpec((None, bq, gw), lambda b, n, i: (b, i, n)),
            pl.BlockSpec((1, hd), lambda b, n, i: (0, 0)),
        ] + ([tab, tab] if rope is not None else []) + [
            pl.BlockSpec((None, S, nkv * hd), lambda b, n, i: (b, 0, n)),
            pl.BlockSpec((None, S, nkv * hd), lambda b, n, i: (b, 0, n)),
        ],
        out_specs=pl.BlockSpec((None, bq, gw), lambda b, n, i: (b, i, n)),
        out_shape=jax.ShapeDtypeStruct((B, Nq, W), BF16),
        compiler_params=_cparams(3),
        name=name,
    )(q, gain.reshape(1, hd), *(rope if rope is not None else ()), k, v)


def _mla_kernel(q_ref, kn_ref, kpe_ref, v_ref, o_ref, *, hb, scale, sub):
    kpe = kpe_ref[...]
    ks = [jnp.concatenate([kn_ref[:, h * LANES:(h + 1) * LANES], kpe], axis=1) for h in range(hb)]
    for h, r, n in _chunks(hb, q_ref.shape[0], sub):
        o = _attend(q_ref[r:r + n, 2 * h * LANES:2 * (h + 1) * LANES], ks[h],
                    v_ref[:, h * LANES:(h + 1) * LANES], scale)
        o_ref[r:r + n, h * LANES:(h + 1) * LANES] = o.astype(o_ref.dtype)


def _mla(q, kv, kpe, heads, scale, name, bq, hb, sub=512):
    B, Nq, _ = q.shape
    S = kv.shape[1]
    bq = _blk(Nq, bq)
    sub = _blk(bq, sub)
    nhb = heads // hb
    return pl.pallas_call(
        functools.partial(_mla_kernel, hb=hb, scale=scale, sub=sub),
        grid=(B, nhb, Nq // bq),
        in_specs=[
            pl.BlockSpec((None, bq, 2 * hb * LANES), lambda b, h, i: (b, i, h)),
            pl.BlockSpec((None, S, hb * LANES), lambda b, h, i: (b, 0, h)),
            pl.BlockSpec((None, S, LANES), lambda b, h, i: (b, 0, 0)),
            pl.BlockSpec((None, S, hb * LANES), lambda b, h, i: (b, 0, nhb + h)),
        ],
        out_specs=pl.BlockSpec((None, bq, hb * LANES), lambda b, h, i: (b, i, h)),
        out_shape=jax.ShapeDtypeStruct((B, Nq, heads * LANES), BF16),
        compiler_params=_cparams(3),
        name=name,
    )(q, kv, kpe, kv)


def _rope_tables(n, dim):
    rows = n // GRID_W
    row = jnp.repeat(jnp.arange(rows, dtype=F32), GRID_W)
    col = jnp.tile(jnp.arange(GRID_W, dtype=F32), rows)
    quarter = dim // 4
    freqs = ROPE_THETA ** (-jnp.arange(quarter, dtype=F32) / quarter)
    ra = row[:, None] * freqs[None, :]
    ca = col[:, None] * freqs[None, :]
    ang = jnp.concatenate([ra, ra, ca, ca], axis=-1)
    sign = jnp.where((jnp.arange(dim) // quarter) % 2 == 0, -1.0, 1.0).astype(F32)
    cos = jnp.cos(ang)
    sin = jnp.sin(ang) * sign[None, :]
    if dim < LANES:
        cos = jnp.pad(cos, ((0, 0), (0, LANES - dim)), constant_values=1.0)
        sin = jnp.pad(sin, ((0, 0), (0, LANES - dim)))
    return cos, sin


def _stacked_weights(dims, w_in, w_s, b_s, w_uq, w_ukv, w_pa, w_pb, w_pc, w_o, w_gate, w_up, w_down):
    A, qw, kvw, qr, rank, rd, D, mh, nope, vd = dims
    L = w_in.shape[0]
    bf = lambda a: a.astype(BF16)
    o_c = 2 * A + qw + 2 * kvw + qr
    o_g = o_c + rank + rd
    s = {}
    s['in'] = bf(w_in[..., :o_c])
    s['ckv'] = bf(jnp.pad(w_in[..., o_c:o_g], ((0, 0), (0, 0), (0, LANES - rd))))
    s['g'] = bf(w_in[..., o_g:])
    uq = w_uq.reshape(L, qr, mh, nope + rd)
    s['uq'] = bf(jnp.pad(uq, ((0, 0), (0, 0), (0, 0), (0, 2 * LANES - nope - rd))).reshape(L, qr, mh * 2 * LANES))
    ukv = w_ukv.reshape(L, rank, mh, nope + vd)
    s['ukv'] = bf(jnp.concatenate([ukv[..., :nope].reshape(L, rank, mh * nope),
                                   ukv[..., nope:].reshape(L, rank, mh * vd)], axis=-1))
    for name, a in (('pa', w_pa), ('pb', w_pb), ('pc', w_pc), ('o', w_o), ('down', w_down), ('s', w_s)):
        s[name] = bf(a)
    s['gate'], s['up'] = w_gate, w_up
    gd = A // w_s.shape[1]
    s['sb'] = jnp.repeat(jnp.swapaxes(b_s, 1, 2), gd, axis=2)
    return s


def _layer_weights(l, dims, s):
    A, qw, kvw, qr, rank, rd, D, mh, nope, vd = dims
    o = [0, 2 * A, 2 * A + qw, 2 * A + qw + 2 * kvw, 2 * A + qw + 2 * kvw + qr]
    full = lambda a: _W(a, l, 0, a.shape[2])
    w = {}
    w['uv'] = _W(s['in'], l, o[0], 2 * A)
    w['q'] = _W(s['in'], l, o[1], qw)
    w['kv'] = _W(s['in'], l, o[2], 2 * kvw)
    w['cq'] = _W(s['in'], l, o[3], qr)
    for name in ('ckv', 'g', 'uq', 'ukv', 'pa', 'pb', 'pc', 'o', 'gate', 'up', 'down'):
        w[name] = full(s[name])
    w['s'] = s['s'][l]
    w['sb'] = s['sb'][l]
    return w


def _trunk_layer(x, h, mods, w, vecs, dims, B, T, rope, ctx, alpha, nxt):
    A, qw, kvw, qr, rank, rd, D, mh, nope, vd = dims
    sgu_g, sgu_b, qg, kg, cqg, ckvg, ln1g, ln1b, ln2g, ln2b = vecs
    M = B * T
    G = mods.shape[0]
    rpg = M // G
    hd = LANES
    kvh = kvw // hd
    tag = "s" if rope is not None else "p"
    rope_g = None if rope is None else rope[0]
    rope_m = None if rope is None else rope[1]

    uv = _mm_plain(h, w['uv'], F32, "proj_uv_" + tag)
    q = _mm_plain(h, w['q'], F32, "proj_q_" + tag)
    kv_f, kv_b = _mm_heads(h, w['kv'], kg, rope_g, T, kvh, [F32, BF16], "proj_kv_" + tag,
                           bm=1024, bn=2 * kvw)
    cq = _mm_rms(h, w['cq'], cqg, BF16, "proj_cq_" + tag)
    ckv_f, ckv_b, kpe_f, kpe_b = _mm_ckv(h, w['ckv'], ckvg, rope_m, T, rank, rd, "proj_ckv_" + tag)
    gates = _mm_sigmoid(h, w['g'], BF16, "proj_gates_" + tag)

    y_a = _sgu(uv, sgu_g, sgu_b, w['s'], w['sb'], "sgu_" + tag)

    k_b = kv_b[:, :kvw].reshape(B, T, kvw)
    v_b = kv_b[:, kvw:].reshape(B, T, kvw)
    if ctx is not None:
        c_k, c_v, c_ckv, c_kpe = ctx
        P = c_k.shape[1]
        k_b = jnp.concatenate([c_k.reshape(B, P, kvw).astype(BF16), k_b], axis=1)
        v_b = jnp.concatenate([c_v.reshape(B, P, kvw).astype(BF16), v_b], axis=1)
    long_keys = ctx is not None
    y_b = _gqa(q.reshape(B, T, qw), qg, rope_g, k_b, v_b, kvh, hd, "gqa_" + tag,
               bq=1024 if long_keys else 256, nkv=1 if long_keys else kvh).reshape(M, qw)

    qc = _mm_uq(cq, w['uq'], rope_m, T, rd, "mla_uq_" + tag, bn=2048)
    ckv_all = ckv_b.reshape(B, T, rank)
    kpe_all = kpe_b.reshape(B, T, LANES)
    if ctx is not None:
        ckv_all = jnp.concatenate([c_ckv.astype(BF16), ckv_all], axis=1)
        c_kpe_pad = jnp.pad(c_kpe, ((0, 0), (0, 0), (0, LANES - rd))).astype(BF16)
        kpe_all = jnp.concatenate([c_kpe_pad, kpe_all], axis=1)
    S = ckv_all.shape[1]
    kvu = _mm_plain(ckv_all.reshape(B * S, rank), w['ukv'], BF16, "mla_ukv_" + tag, bn=4096)
    y_c = _mla(qc.reshape(B, T, mh * 2 * LANES), kvu.reshape(B, S, mh * (nope + vd)), kpe_all,
               mh, (nope + rd) ** -0.5, "mla_" + tag,
               bq=4096 if long_keys else 256, hb=1 if long_keys else mh).reshape(M, mh * vd)

    merged = _mm_merge(y_a, y_b, y_c, w['pa'], w['pb'], w['pc'], gates, "merge_" + tag)
    mix = _mm_plain(merged, w['o'], BF16, "proj_o_" + tag)
    x1, h2 = _ln_residual(x, mix, mods, 2, ln1g, ln1b, alpha, rpg, nxt=(mods, 4, 3))
    hid = _mm_swiglu(h2, w['gate'], w['up'], "ffn_in_" + tag)
    ff = _mm_plain(hid, w['down'], BF16, "ffn_out_" + tag, bm=512, bn=512)
    x2, h_next = _ln_residual(x1, ff, mods, 5, ln2g, ln2b, alpha, rpg, nxt=nxt)

    own = (kv_f[:, :kvw], kv_f[:, kvw:], ckv_f, kpe_f[:, :rd])
    return x2, h_next, own


def kernel(x_prompt, x_sample, cache_k, cache_v, cache_ckv, cache_kpe, c, c_ctx,
           w_ada, b_ada, w_in, sgu_ln_g, sgu_ln_b, w_s, b_s, q_norm_g, k_norm_g,
           mla_q_norm_g, mla_kv_norm_g, w_uq, w_ukv, w_pa, w_pb, w_pc, w_o,
           ln1_g, ln1_b, ln2_g, ln2_b, w_gate, w_up, w_down):
    Bp, Tp, D = x_prompt.shape
    Bs, Ts, _ = x_sample.shape
    L = w_ada.shape[0]
    A = sgu_ln_g.shape[1]
    hd = q_norm_g.shape[1]
    kvh = cache_k.shape[3]
    qw = w_pb.shape[1]
    kvw = kvh * hd
    qr = mla_q_norm_g.shape[1]
    rank = mla_kv_norm_g.shape[1]
    rd = cache_kpe.shape[-1]
    vd = LANES
    mh = w_pc.shape[1] // vd
    nope = w_uq.shape[2] // mh - rd
    assert hd == LANES and nope == LANES and w_ukv.shape[2] == mh * (nope + vd)
    dims = (A, qw, kvw, qr, rank, rd, D, mh, nope, vd)
    alpha = float((2 * L) ** 0.25)

    R = -(-(1 + Bs) // 8) * 8
    cond = jnp.concatenate([c_ctx[None, :], c, jnp.zeros((R - 1 - Bs, D), F32)], axis=0)
    mods_all = _ada(cond, w_ada, b_ada)

    rope = (_rope_tables(Ts, hd), _rope_tables(Ts, rd))

    xp = x_prompt.reshape(Bp * Tp, D)
    xs = x_sample.reshape(Bs * Ts, D)
    mods = [(mods_all[l, 0:1].reshape(1, 1, 6 * D), mods_all[l, 1:1 + Bs].reshape(Bs, 1, 6 * D))
            for l in range(L)]
    hp = _modulate(xp, mods[0][0], 1, 0, Bp * Tp)
    hs = _modulate(xs, mods[0][1], 1, 0, Ts)

    sw = _stacked_weights(dims, w_in, w_s, b_s, w_uq, w_ukv, w_pa, w_pb, w_pc, w_o, w_gate, w_up, w_down)
    new_k, new_v, new_ckv, new_kpe = [], [], [], []
    for l in range(L):
        w = _layer_weights(l, dims, sw)
        vecs = (sgu_ln_g[l], sgu_ln_b[l], q_norm_g[l], k_norm_g[l], mla_q_norm_g[l], mla_kv_norm_g[l],
                ln1_g[l], ln1_b[l], ln2_g[l], ln2_b[l])
        nxt_p = (mods[l + 1][0], 1, 0) if l + 1 < L else None
        nxt_s = (mods[l + 1][1], 1, 0) if l + 1 < L else None
        xp, hp, own = _trunk_layer(xp, hp, mods[l][0], w, vecs, dims, Bp, Tp, None, None, alpha, nxt_p)
        new_k.append(own[0].reshape(Bp, Tp, kvh, hd))
        new_v.append(own[1].reshape(Bp, Tp, kvh, hd))
        new_ckv.append(own[2].reshape(Bp, Tp, rank))
        new_kpe.append(own[3].reshape(Bp, Tp, rd))
        ctx = (cache_k[:, l], cache_v[:, l], cache_ckv[:, l], cache_kpe[:, l])
        xs, hs, _ = _trunk_layer(xs, hs, mods[l][1], w, vecs, dims, Bs, Ts, rope, ctx, alpha, nxt_s)

    return (xp.reshape(Bp, Tp, D), xs.reshape(Bs, Ts, D),
            jnp.stack(new_k, axis=1), jnp.stack(new_v, axis=1),
            jnp.stack(new_ckv, axis=1), jnp.stack(new_kpe, axis=1))
```

```python
import functools
from typing import NamedTuple

import jax
import jax.numpy as jnp
from jax import lax
from jax.experimental import pallas as pl
from jax.experimental.pallas import tpu as pltpu

F32 = jnp.float32
BF16 = jnp.bfloat16

NORM_EPS = 1e-6
ROPE_THETA = 10000.0
GRID_W = 64

LANES = 128
VMEM_LIMIT_BYTES = 56 * 1024 * 1024


def _cparams(n_axes):
    return pltpu.CompilerParams(
        dimension_semantics=("arbitrary",) * n_axes,
        vmem_limit_bytes=VMEM_LIMIT_BYTES,
    )


def _blk(n, pref):
    b = min(n, pref)
    while n % b:
        b //= 2
    return b


def _rms(a, g):
    ms = jnp.mean(a * a, axis=-1, keepdims=True)
    return a * lax.rsqrt(ms + NORM_EPS) * g


def _rope(y, cos, sin_signed, quarter):
    n = y.shape[-1]
    lane = lax.broadcasted_iota(jnp.int32, y.shape, 1)
    first = (lane & quarter) == 0
    rot = jnp.where(first, pltpu.roll(y, n - quarter, axis=1), pltpu.roll(y, quarter, axis=1))
    return y * cos + rot * sin_signed


def _sigmoid(x):
    return 0.5 * jnp.tanh(0.5 * x) + 0.5


def _layer_norm(z, g, b):
    mu = jnp.mean(z, axis=-1, keepdims=True)
    zc = z - mu
    var = jnp.mean(zc * zc, axis=-1, keepdims=True)
    return zc * lax.rsqrt(var + NORM_EPS) * g + b


def _ada_kernel(c_ref, w_ref, b_ref, o_ref):
    c = c_ref[...]
    s = (c * _sigmoid(c)).astype(BF16)
    o_ref[...] = jnp.dot(s, w_ref[...].astype(BF16), preferred_element_type=F32) + b_ref[...]


def _ada(cond, w_ada, b_ada):
    L, D, N = w_ada.shape
    R = cond.shape[0]
    bn = _blk(N, 512)
    return pl.pallas_call(
        _ada_kernel,
        grid=(L, N // bn),
        in_specs=[
            pl.BlockSpec((R, D), lambda l, j: (0, 0)),
            pl.BlockSpec((None, D, bn), lambda l, j: (l, 0, j)),
            pl.BlockSpec((None, 1, bn), lambda l, j: (l, 0, j)),
        ],
        out_specs=pl.BlockSpec((None, R, bn), lambda l, j: (l, 0, j)),
        out_shape=jax.ShapeDtypeStruct((L, R, N), F32),
        compiler_params=_cparams(2),
        name="ada_mod",
    )(cond, w_ada, b_ada.reshape(L, 1, N))


def _modulate_kernel(x_ref, sc_ref, sh_ref, h_ref):
    h_ref[...] = (x_ref[...] * (1.0 + sc_ref[...]) + sh_ref[...]).astype(h_ref.dtype)


def _mod_spec(D, k, rows_per_group, bt):
    return pl.BlockSpec((None, 1, D), lambda i, *_: ((i * bt) // rows_per_group, 0, k))


def _modulate(x, mods, k_sc, k_sh, rows_per_group):
    M, D = x.shape
    bt = _blk(rows_per_group, 512)
    return pl.pallas_call(
        _modulate_kernel,
        grid=(M // bt,),
        in_specs=[
            pl.BlockSpec((bt, D), lambda i: (i, 0)),
            _mod_spec(D, k_sc, rows_per_group, bt),
            _mod_spec(D, k_sh, rows_per_group, bt),
        ],
        out_specs=pl.BlockSpec((bt, D), lambda i: (i, 0)),
        out_shape=jax.ShapeDtypeStruct((M, D), BF16),
        compiler_params=_cparams(1),
        name="modulate",
    )(x, mods, mods)


def _ln_kernel(x_ref, y_ref, gate_ref, g_ref, b_ref, *rest, alpha, with_h):
    z = alpha * x_ref[...] + gate_ref[...] * y_ref[...].astype(F32)
    xn = _layer_norm(z, g_ref[...], b_ref[...])
    if with_h:
        sc_ref, sh_ref, xo_ref, h_ref = rest
        xo_ref[...] = xn
        h_ref[...] = (xn * (1.0 + sc_ref[...]) + sh_ref[...]).astype(h_ref.dtype)
    else:
        (xo_ref,) = rest
        xo_ref[...] = xn


def _ln_residual(x, y, mods, k_gate, ln_g, ln_b, alpha, rows_per_group, nxt=None):
    M, D = x.shape
    bt = _blk(rows_per_group, 256)
    row = pl.BlockSpec((bt, D), lambda i: (i, 0))
    vec = pl.BlockSpec((1, D), lambda i: (0, 0))
    in_specs = [row, row, _mod_spec(D, k_gate, rows_per_group, bt), vec, vec]
    args = [x, y, mods, ln_g.reshape(1, D), ln_b.reshape(1, D)]
    out_shape = [jax.ShapeDtypeStruct((M, D), F32)]
    out_specs = [row]
    if nxt is not None:
        mods_n, k_sc, k_sh = nxt
        in_specs += [_mod_spec(D, k_sc, rows_per_group, bt), _mod_spec(D, k_sh, rows_per_group, bt)]
        args += [mods_n, mods_n]
        out_shape.append(jax.ShapeDtypeStruct((M, D), BF16))
        out_specs.append(row)
    res = pl.pallas_call(
        functools.partial(_ln_kernel, alpha=alpha, with_h=nxt is not None),
        grid=(M // bt,),
        in_specs=in_specs,
        out_specs=out_specs,
        out_shape=out_shape,
        compiler_params=_cparams(1),
        name="ln_residual",
    )(*args)
    return res if nxt is not None else (res[0], None)


class _W(NamedTuple):
    arr: jax.Array
    layer: int
    col0: int
    n_cols: int
    t: bool = False


def _wfull(a):
    return _W(a[None], 0, 0, a.shape[1])


def _wspec(w, bn):
    l, ob = w.layer, w.col0 // bn
    assert w.col0 % bn == 0 and w.n_cols % bn == 0
    if w.t:
        return pl.BlockSpec((None, bn, w.arr.shape[2]), lambda i, j: (l, ob + j, 0))
    return pl.BlockSpec((None, w.arr.shape[1], bn), lambda i, j: (l, 0, ob + j))


def _mm_call(kernel, x, ws, bm, bn, extras, extra_specs, out_dtypes, name):
    M, K = x.shape
    n_cols = ws[0].n_cols
    in_specs = [pl.BlockSpec((bm, K), lambda i, j: (i, 0))]
    in_specs += [_wspec(w, bn) for w in ws]
    in_specs += list(extra_specs)
    out_shape = [jax.ShapeDtypeStruct((M, n_cols), dt) for dt in out_dtypes]
    out_specs = [pl.BlockSpec((bm, bn), lambda i, j: (i, j)) for _ in out_dtypes]
    return pl.pallas_call(
        kernel,
        grid=(M // bm, n_cols // bn),
        in_specs=in_specs,
        out_specs=out_specs,
        out_shape=out_shape,
        compiler_params=_cparams(2),
        name=name,
    )(x, *[w.arr for w in ws], *extras)


_NT = (((1,), (1,)), ((), ()))


def _xw(x_ref, w_ref, wt=False):
    w = w_ref[...].astype(BF16)
    if wt:
        return lax.dot_general(x_ref[...], w, _NT, preferred_element_type=F32)
    return jnp.dot(x_ref[...], w, preferred_element_type=F32)


def _mm_plain_kernel(x_ref, w_ref, o_ref, *, wt):
    o_ref[...] = _xw(x_ref, w_ref, wt).astype(o_ref.dtype)


def _mm_plain(x, w, out_dtype, name, bm=1024, bn=1024):
    bm, bn = _blk(x.shape[0], bm), _blk(w.n_cols, bn)
    kern = functools.partial(_mm_plain_kernel, wt=w.t)
    return _mm_call(kern, x, [w], bm, bn, [], [], [out_dtype], name)[0]


def _rope_specs(rope, bm, rows_per_batch):
    if rope is None:
        return [], []
    nb = rows_per_batch // bm
    spec = pl.BlockSpec((bm, LANES), lambda i, j: (i % nb, 0))
    return [rope[0], rope[1]], [spec, spec]


def _mm_heads_kernel(x_ref, w_ref, g_ref, *rest, n_norm, rope, quarter, wt):
    if rope:
        cos_ref, sin_ref, *outs = rest
    else:
        outs = rest
    acc = _xw(x_ref, w_ref, wt)
    for h in range(acc.shape[1] // LANES):
        a = acc[:, h * LANES:(h + 1) * LANES]
        if h < n_norm:
            a = _rms(a, g_ref[...])
            if rope:
                a = _rope(a, cos_ref[...], sin_ref[...], quarter)
        for o in outs:
            o[:, h * LANES:(h + 1) * LANES] = a.astype(o.dtype)


def _mm_heads(x, w, gain, rope, rows_per_batch, n_norm, out_dtypes, name, bm, bn):
    bm, bn = _blk(x.shape[0] if rope is None else rows_per_batch, bm), _blk(w.n_cols, bn)
    rargs, rspecs = _rope_specs(rope, bm, rows_per_batch)
    kern = functools.partial(_mm_heads_kernel, n_norm=n_norm, rope=rope is not None,
                             quarter=LANES // 4, wt=w.t)
    return _mm_call(kern, x, [w], bm, bn,
                    [gain.reshape(1, LANES)] + rargs,
                    [pl.BlockSpec((1, LANES), lambda i, j: (0, 0))] + rspecs,
                    out_dtypes, name)


def _mm_rms_kernel(x_ref, w_ref, g_ref, o_ref, *, wt):
    o_ref[...] = _rms(_xw(x_ref, w_ref, wt), g_ref[...]).astype(o_ref.dtype)


def _mm_rms(x, w, gain, out_dtype, name, bm=1024):
    N = w.n_cols
    bm = _blk(x.shape[0], bm)
    return _mm_call(functools.partial(_mm_rms_kernel, wt=w.t), x, [w], bm, N, [gain.reshape(1, N)],
                    [pl.BlockSpec((1, N), lambda i, j: (0, 0))], [out_dtype], name)[0]


def _mm_ckv_kernel(x_ref, w_ref, g_ref, *rest, rank, rope_dim, rope, wt):
    if rope:
        cos_ref, sin_ref, ckv_f, ckv_b, kpe_f, kpe_b = rest
    else:
        ckv_f, ckv_b, kpe_f, kpe_b = rest
    if wt:
        acc = lax.dot_general(x_ref[...], w_ref[:rank + LANES, :].astype(BF16), _NT, preferred_element_type=F32)
    else:
        acc = jnp.dot(x_ref[...], w_ref[:, :rank + LANES].astype(BF16), preferred_element_type=F32)
    ckv = _rms(acc[:, :rank], g_ref[...])
    kpe = acc[:, rank:rank + LANES]
    lane = lax.broadcasted_iota(jnp.int32, kpe.shape, 1)
    kpe = jnp.where(lane < rope_dim, kpe, 0.0)
    if rope:
        kpe = _rope(kpe, cos_ref[...], sin_ref[...], rope_dim // 4)
    ckv_f[...] = ckv
    ckv_b[...] = ckv.astype(BF16)
    kpe_f[...] = kpe
    kpe_b[...] = kpe.astype(BF16)


def _mm_ckv(x, w, gain, rope, rows_per_batch, rank, rope_dim, name, bm=1024):
    M, K = x.shape
    bw = w.n_cols
    assert bw >= rank + LANES and w.col0 % bw == 0
    bm = _blk(M if rope is None else rows_per_batch, bm)
    rargs, rspecs = _rope_specs(rope, bm, rows_per_batch)
    kern = functools.partial(_mm_ckv_kernel, rank=rank, rope_dim=rope_dim, rope=rope is not None, wt=w.t)
    in_specs = [pl.BlockSpec((bm, K), lambda i, j: (i, 0)),
                _wspec(w, bw),
                pl.BlockSpec((1, rank), lambda i, j: (0, 0))] + rspecs
    out_shape = [jax.ShapeDtypeStruct((M, rank), F32), jax.ShapeDtypeStruct((M, rank), BF16),
                 jax.ShapeDtypeStruct((M, LANES), F32), jax.ShapeDtypeStruct((M, LANES), BF16)]
    out_specs = [pl.BlockSpec((bm, rank), lambda i, j: (i, 0)), pl.BlockSpec((bm, rank), lambda i, j: (i, 0)),
                 pl.BlockSpec((bm, LANES), lambda i, j: (i, 0)), pl.BlockSpec((bm, LANES), lambda i, j: (i, 0))]
    return pl.pallas_call(
        kern, grid=(M // bm, 1), in_specs=in_specs, out_specs=out_specs, out_shape=out_shape,
        compiler_params=_cparams(2), name=name,
    )(x, w.arr, gain.reshape(1, rank), *rargs)


def _mm_uq_kernel(x_ref, w_ref, *rest, rope, quarter):
    if rope:
        cos_ref, sin_ref, o_ref = rest
    else:
        (o_ref,) = rest
    acc = _xw(x_ref, w_ref)
    for h in range(acc.shape[1] // LANES):
        a = acc[:, h * LANES:(h + 1) * LANES]
        if rope and h % 2 == 1:
            a = _rope(a, cos_ref[...], sin_ref[...], quarter)
        o_ref[:, h * LANES:(h + 1) * LANES] = a.astype(o_ref.dtype)


def _mm_uq(x, w, rope, rows_per_batch, rope_dim, name, bm=1024, bn=1024):
    bm, bn = _blk(x.shape[0] if rope is None else rows_per_batch, bm), _blk(w.n_cols, bn)
    rargs, rspecs = _rope_specs(rope, bm, rows_per_batch)
    kern = functools.partial(_mm_uq_kernel, rope=rope is not None, quarter=rope_dim // 4)
    return _mm_call(kern, x, [w], bm, bn, rargs, rspecs, [BF16], name)[0]


def _mm_sigmoid_kernel(x_ref, w_ref, o_ref, *, wt):
    o_ref[...] = _sigmoid(_xw(x_ref, w_ref, wt)).astype(o_ref.dtype)


def _mm_sigmoid(x, w, out_dtype, name, bm=1024, bn=1024):
    bm, bn = _blk(x.shape[0], bm), _blk(w.n_cols, bn)
    return _mm_call(functools.partial(_mm_sigmoid_kernel, wt=w.t), x, [w], bm, bn, [], [], [out_dtype], name)[0]


def _mm_swiglu_kernel(x_ref, wg_ref, wu_ref, o_ref):
    a = _xw(x_ref, wg_ref)
    b = _xw(x_ref, wu_ref)
    o_ref[...] = (a * _sigmoid(a) * b).astype(o_ref.dtype)


def _mm_swiglu(x, wg, wu, name, bm=1024, bn=256):
    bm, bn = _blk(x.shape[0], bm), _blk(wg.n_cols, bn)
    return _mm_call(_mm_swiglu_kernel, x, [wg, wu], bm, bn, [], [], [BF16], name)[0]


def _mm_merge_kernel(ya_ref, yb_ref, yc_ref, wa_ref, wb_ref, wc_ref, ga_ref, gb_ref, gc_ref, o_ref):
    a = jnp.dot(ya_ref[...], wa_ref[...], preferred_element_type=F32)
    b = jnp.dot(yb_ref[...], wb_ref[...], preferred_element_type=F32)
    c = jnp.dot(yc_ref[...], wc_ref[...], preferred_element_type=F32)
    ga, gb, gc = (g[...].astype(F32) for g in (ga_ref, gb_ref, gc_ref))
    o_ref[...] = (ga * a + gb * b + gc * c).astype(o_ref.dtype)


def _mm_merge(ya, yb, yc, wa, wb, wc, gates, name, bm=1024, bn=512):
    M = ya.shape[0]
    D = wa.n_cols
    bm, bn = _blk(M, bm), _blk(D, bn)
    nb = D // bn
    xs = lambda y: pl.BlockSpec((bm, y.shape[1]), lambda i, j: (i, 0))
    gsp = lambda t: pl.BlockSpec((bm, bn), lambda i, j: (i, j + t * nb))
    return pl.pallas_call(
        _mm_merge_kernel,
        grid=(M // bm, nb),
        in_specs=[xs(ya), xs(yb), xs(yc), _wspec(wa, bn), _wspec(wb, bn), _wspec(wc, bn),
                  gsp(0), gsp(1), gsp(2)],
        out_specs=pl.BlockSpec((bm, bn), lambda i, j: (i, j)),
        out_shape=jax.ShapeDtypeStruct((M, D), BF16),
        compiler_params=_cparams(2),
        name=name,
    )(ya, yb, yc, wa.arr, wb.arr, wc.arr, gates, gates, gates)


def _sgu_kernel(u_ref, v_ref, lg_ref, lb_ref, ws_ref, bias_ref, o_ref, *, chunk, groups):
    vn = _layer_norm(v_ref[...], lg_ref[...], lb_ref[...]).astype(BF16)
    gd = vn.shape[1] // groups
    for c in range(vn.shape[0] // chunk):
        r0 = c * chunk
        for g in range(groups):
            c0 = g * gd
            s = jnp.dot(ws_ref[g], vn[r0:r0 + chunk, c0:c0 + gd], preferred_element_type=F32)
            s = s + bias_ref[:, c0:c0 + gd]
            o_ref[r0:r0 + chunk, c0:c0 + gd] = (u_ref[r0:r0 + chunk, c0:c0 + gd] * s).astype(o_ref.dtype)


def _sgu(uv, ln_g, ln_b, w_s, bias_full, name):
    M = uv.shape[0]
    A = uv.shape[1] // 2
    G, C, _ = w_s.shape
    bt = _blk(M, 2 * C)
    return pl.pallas_call(
        functools.partial(_sgu_kernel, chunk=C, groups=G),
        grid=(M // bt,),
        in_specs=[
            pl.BlockSpec((bt, A), lambda i: (i, 0)),
            pl.BlockSpec((bt, A), lambda i: (i, 1)),
            pl.BlockSpec((1, A), lambda i: (0, 0)),
            pl.BlockSpec((1, A), lambda i: (0, 0)),
            pl.BlockSpec((G, C, C), lambda i: (0, 0, 0)),
            pl.BlockSpec((C, A), lambda i: (0, 0)),
        ],
        out_specs=pl.BlockSpec((bt, A), lambda i: (i, 0)),
        out_shape=jax.ShapeDtypeStruct((M, A), BF16),
        compiler_params=_cparams(1),
        name=name,
    )(uv, uv, ln_g.reshape(1, A), ln_b.reshape(1, A), w_s, bias_full)


_LOG2E = 1.4426950408889634


def _attend(q, k, v, scale):
    s = lax.dot_general(q, k, _NT, preferred_element_type=F32)
    m = jnp.max(s, axis=-1, keepdims=True)
    p = jnp.exp2((s - m) * (scale * _LOG2E))
    l = jnp.sum(p, axis=-1, keepdims=True)
    return jnp.dot(p.astype(BF16), v, preferred_element_type=F32) / l


def _chunks(n_heads, rows, sub):
    out = [(h, r, sub) for h in range(n_heads) for r in range(0, rows, sub)]
    if sub >= 512 and len(out) >= 4:
        h, r, _ = out[0]
        out[0:1] = [(h, r, sub // 4), (h, r + sub // 4, sub - sub // 4)]
        h, r, _ = out[-1]
        out[-1:] = [(h, r, sub - sub // 4), (h, r + sub - sub // 4, sub // 4)]
    return out


def _gqa_kernel(q_ref, g_ref, *rest, nkv, group, hd, scale, sub, rope):
    if rope:
        cos_ref, sin_ref, k_ref, v_ref, o_ref = rest
    else:
        k_ref, v_ref, o_ref = rest
    for h, r, n in _chunks(nkv * group, q_ref.shape[0], sub):
        kv = h // group
        q = _rms(q_ref[r:r + n, h * hd:(h + 1) * hd], g_ref[...])
        if rope:
            q = _rope(q, cos_ref[r:r + n, :], sin_ref[r:r + n, :], hd // 4)
        o = _attend(q.astype(BF16), k_ref[:, kv * hd:(kv + 1) * hd],
                    v_ref[:, kv * hd:(kv + 1) * hd], scale)
        o_ref[r:r + n, h * hd:(h + 1) * hd] = o.astype(o_ref.dtype)


def _gqa(q, gain, rope, k, v, kv_heads, hd, name, bq, nkv, sub=512):
    B, Nq, W = q.shape
    S = k.shape[1]
    group = W // (kv_heads * hd)
    bq = _blk(Nq, bq)
    gw = nkv * group * hd
    sub = _blk(bq, sub)
    tab = pl.BlockSpec((bq, hd), lambda b, n, i: (i, 0))
    return pl.pallas_call(
        functools.partial(_gqa_kernel, nkv=nkv, group=group, hd=hd, scale=hd ** -0.5, sub=sub,
                          rope=rope is not None),
        grid=(B, kv_heads // nkv, Nq // bq),
        in_specs=[
            pl.BlockSpec((None, bq, gw), lambda b, n, i: (b, i, n)),
            pl.BlockSpec((1, hd), lambda b, n, i: (0, 0)),
        ] + ([tab, tab] if rope is not None else []) + [
            pl.BlockSpec((None, S, nkv * hd), lambda b, n, i: (b, 0, n)),
            pl.BlockSpec((None, S, nkv * hd), lambda b, n, i: (b, 0, n)),
        ],
        out_specs=pl.BlockSpec((None, bq, gw), lambda b, n, i: (b, i, n)),
        out_shape=jax.ShapeDtypeStruct((B, Nq, W), BF16),
        compiler_params=_cparams(3),
        name=name,
    )(q, gain.reshape(1, hd), *(rope if rope is not None else ()), k, v)


def _mla_kernel(q_ref, kn_ref, kpe_ref, v_ref, o_ref, *, hb, scale, sub):
    kpe = kpe_ref[...]
    ks = [jnp.concatenate([kn_ref[:, h * LANES:(h + 1) * LANES], kpe], axis=1) for h in range(hb)]
    for h, r, n in _chunks(hb, q_ref.shape[0], sub):
        o = _attend(q_ref[r:r + n, 2 * h * LANES:2 * (h + 1) * LANES], ks[h],
                    v_ref[:, h * LANES:(h + 1) * LANES], scale)
        o_ref[r:r + n, h * LANES:(h + 1) * LANES] = o.astype(o_ref.dtype)


def _mla(q, kv, kpe, heads, scale, name, bq, hb, sub=512):
    B, Nq, _ = q.shape
    S = kv.shape[1]
    bq = _blk(Nq, bq)
    sub = _blk(bq, sub)
    nhb = heads // hb
    return pl.pallas_call(
        functools.partial(_mla_kernel, hb=hb, scale=scale, sub=sub),
        grid=(B, nhb, Nq // bq),
        in_specs=[
            pl.BlockSpec((None, bq, 2 * hb * LANES), lambda b, h, i: (b, i, h)),
            pl.BlockSpec((None, S, hb * LANES), lambda b, h, i: (b, 0, h)),
            pl.BlockSpec((None, S, LANES), lambda b, h, i: (b, 0, 0)),
            pl.BlockSpec((None, S, hb * LANES), lambda b, h, i: (b, 0, nhb + h)),
        ],
        out_specs=pl.BlockSpec((None, bq, hb * LANES), lambda b, h, i: (b, i, h)),
        out_shape=jax.ShapeDtypeStruct((B, Nq, heads * LANES), BF16),
        compiler_params=_cparams(3),
        name=name,
    )(q, kv, kpe, kv)


def _rope_tables(n, dim):
    rows = n // GRID_W
    row = jnp.repeat(jnp.arange(rows, dtype=F32), GRID_W)
    col = jnp.tile(jnp.arange(GRID_W, dtype=F32), rows)
    quarter = dim // 4
    freqs = ROPE_THETA ** (-jnp.arange(quarter, dtype=F32) / quarter)
    ra = row[:, None] * freqs[None, :]
    ca = col[:, None] * freqs[None, :]
    ang = jnp.concatenate([ra, ra, ca, ca], axis=-1)
    sign = jnp.where((jnp.arange(dim) // quarter) % 2 == 0, -1.0, 1.0).astype(F32)
    cos = jnp.cos(ang)
    sin = jnp.sin(ang) * sign[None, :]
    if dim < LANES:
        cos = jnp.pad(cos, ((0, 0), (0, LANES - dim)), constant_values=1.0)
        sin = jnp.pad(sin, ((0, 0), (0, LANES - dim)))
    return cos, sin


def _stacked_weights(dims, w_in, w_s, b_s, w_uq, w_ukv, w_pa, w_pb, w_pc, w_o, w_gate, w_up, w_down):
    A, qw, kvw, qr, rank, rd, D, mh, nope, vd = dims
    L = w_in.shape[0]
    bf = lambda a: a.astype(BF16)
    o_c = 2 * A + qw + 2 * kvw + qr
    o_g = o_c + rank + rd
    s = {}
    w_in_t = jnp.swapaxes(w_in, 1, 2)
    s['in'] = bf(w_in_t[:, :o_c])
    s['ckv'] = bf(jnp.pad(w_in_t[:, o_c:o_g], ((0, 0), (0, LANES - rd), (0, 0))))
    s['g'] = bf(w_in_t[:, o_g:])
    uq = w_uq.reshape(L, qr, mh, nope + rd)
    s['uq'] = bf(jnp.pad(uq, ((0, 0), (0, 0), (0, 0), (0, 2 * LANES - nope - rd))).reshape(L, qr, mh * 2 * LANES))
    ukv = w_ukv.reshape(L, rank, mh, nope + vd)
    s['ukv'] = bf(jnp.concatenate([ukv[..., :nope].reshape(L, rank, mh * nope),
                                   ukv[..., nope:].reshape(L, rank, mh * vd)], axis=-1))
    for name, a in (('pa', w_pa), ('pb', w_pb), ('pc', w_pc), ('o', w_o), ('down', w_down), ('s', w_s)):
        s[name] = bf(a)
    s['gate'], s['up'] = w_gate, w_up
    gd = A // w_s.shape[1]
    s['sb'] = jnp.repeat(jnp.swapaxes(b_s, 1, 2), gd, axis=2)
    return s


def _layer_weights(l, dims, s):
    A, qw, kvw, qr, rank, rd, D, mh, nope, vd = dims
    o = [0, 2 * A, 2 * A + qw, 2 * A + qw + 2 * kvw, 2 * A + qw + 2 * kvw + qr]
    full = lambda a: _W(a, l, 0, a.shape[2])
    w = {}
    w['uv'] = _W(s['in'], l, o[0], 2 * A, True)
    w['q'] = _W(s['in'], l, o[1], qw, True)
    w['kv'] = _W(s['in'], l, o[2], 2 * kvw, True)
    w['cq'] = _W(s['in'], l, o[3], qr, True)
    w['ckv'] = _W(s['ckv'], l, 0, s['ckv'].shape[1], True)
    w['g'] = _W(s['g'], l, 0, s['g'].shape[1], True)
    for name in ('uq', 'ukv', 'pa', 'pb', 'pc', 'o', 'gate', 'up', 'down'):
        w[name] = full(s[name])
    w['s'] = s['s'][l]
    w['sb'] = s['sb'][l]
    return w


def _trunk_layer(x, h, mods, w, vecs, dims, B, T, rope, ctx, alpha, nxt):
    A, qw, kvw, qr, rank, rd, D, mh, nope, vd = dims
    sgu_g, sgu_b, qg, kg, cqg, ckvg, ln1g, ln1b, ln2g, ln2b = vecs
    M = B * T
    G = mods.shape[0]
    rpg = M // G
    hd = LANES
    kvh = kvw // hd
    tag = "s" if rope is not None else "p"
    rope_g = None if rope is None else rope[0]
    rope_m = None if rope is None else rope[1]

    uv = _mm_plain(h, w['uv'], F32, "proj_uv_" + tag)
    q = _mm_plain(h, w['q'], F32, "proj_q_" + tag)
    kv_f, kv_b = _mm_heads(h, w['kv'], kg, rope_g, T, kvh, [F32, BF16], "proj_kv_" + tag,
                           bm=1024, bn=2 * kvw)
    cq = _mm_rms(h, w['cq'], cqg, BF16, "proj_cq_" + tag)
    ckv_f, ckv_b, kpe_f, kpe_b = _mm_ckv(h, w['ckv'], ckvg, rope_m, T, rank, rd, "proj_ckv_" + tag)
    gates = _mm_sigmoid(h, w['g'], BF16, "proj_gates_" + tag)

    y_a = _sgu(uv, sgu_g, sgu_b, w['s'], w['sb'], "sgu_" + tag)

    k_b = kv_b[:, :kvw].reshape(B, T, kvw)
    v_b = kv_b[:, kvw:].reshape(B, T, kvw)
    if ctx is not None:
        c_k, c_v, c_ckv, c_kpe = ctx
        P = c_k.shape[1]
        k_b = jnp.concatenate([c_k.reshape(B, P, kvw).astype(BF16), k_b], axis=1)
        v_b = jnp.concatenate([c_v.reshape(B, P, kvw).astype(BF16), v_b], axis=1)
    long_keys = ctx is not None
    y_b = _gqa(q.reshape(B, T, qw), qg, rope_g, k_b, v_b, kvh, hd, "gqa_" + tag,
               bq=1024 if long_keys else 256, nkv=1 if long_keys else kvh).reshape(M, qw)

    qc = _mm_uq(cq, w['uq'], rope_m, T, rd, "mla_uq_" + tag, bn=2048)
    ckv_all = ckv_b.reshape(B, T, rank)
    kpe_all = kpe_b.reshape(B, T, LANES)
    if ctx is not None:
        ckv_all = jnp.concatenate([c_ckv.astype(BF16), ckv_all], axis=1)
        c_kpe_pad = jnp.pad(c_kpe, ((0, 0), (0, 0), (0, LANES - rd))).astype(BF16)
        kpe_all = jnp.concatenate([c_kpe_pad, kpe_all], axis=1)
    S = ckv_all.shape[1]
    kvu = _mm_plain(ckv_all.reshape(B * S, rank), w['ukv'], BF16, "mla_ukv_" + tag, bn=4096)
    y_c = _mla(qc.reshape(B, T, mh * 2 * LANES), kvu.reshape(B, S, mh * (nope + vd)), kpe_all,
               mh, (nope + rd) ** -0.5, "mla_" + tag,
               bq=4096 if long_keys else 256, hb=1 if long_keys else mh).reshape(M, mh * vd)

    merged = _mm_merge(y_a, y_b, y_c, w['pa'], w['pb'], w['pc'], gates, "merge_" + tag)
    mix = _mm_plain(merged, w['o'], BF16, "proj_o_" + tag)
    x1, h2 = _ln_residual(x, mix, mods, 2, ln1g, ln1b, alpha, rpg, nxt=(mods, 4, 3))
    hid = _mm_swiglu(h2, w['gate'], w['up'], "ffn_in_" + tag)
    ff = _mm_plain(hid, w['down'], BF16, "ffn_out_" + tag, bm=512, bn=512)
    x2, h_next = _ln_residual(x1, ff, mods, 5, ln2g, ln2b, alpha, rpg, nxt=nxt)

    own = (kv_f[:, :kvw], kv_f[:, kvw:], ckv_f, kpe_f[:, :rd])
    return x2, h_next, own


def kernel(x_prompt, x_sample, cache_k, cache_v, cache_ckv, cache_kpe, c, c_ctx,
           w_ada, b_ada, w_in, sgu_ln_g, sgu_ln_b, w_s, b_s, q_norm_g, k_norm_g,
           mla_q_norm_g, mla_kv_norm_g, w_uq, w_ukv, w_pa, w_pb, w_pc, w_o,
           ln1_g, ln1_b, ln2_g, ln2_b, w_gate, w_up, w_down):
    Bp, Tp, D = x_prompt.shape
    Bs, Ts, _ = x_sample.shape
    L = w_ada.shape[0]
    A = sgu_ln_g.shape[1]
    hd = q_norm_g.shape[1]
    kvh = cache_k.shape[3]
    qw = w_pb.shape[1]
    kvw = kvh * hd
    qr = mla_q_norm_g.shape[1]
    rank = mla_kv_norm_g.shape[1]
    rd = cache_kpe.shape[-1]
    vd = LANES
    mh = w_pc.shape[1] // vd
    nope = w_uq.shape[2] // mh - rd
    assert hd == LANES and nope == LANES and w_ukv.shape[2] == mh * (nope + vd)
    dims = (A, qw, kvw, qr, rank, rd, D, mh, nope, vd)
    alpha = float((2 * L) ** 0.25)

    R = -(-(1 + Bs) // 8) * 8
    cond = jnp.concatenate([c_ctx[None, :], c, jnp.zeros((R - 1 - Bs, D), F32)], axis=0)
    mods_all = _ada(cond, w_ada, b_ada)

    rope = (_rope_tables(Ts, hd), _rope_tables(Ts, rd))

    xp = x_prompt.reshape(Bp * Tp, D)
    xs = x_sample.reshape(Bs * Ts, D)
    mods = [(mods_all[l, 0:1].reshape(1, 1, 6 * D), mods_all[l, 1:1 + Bs].reshape(Bs, 1, 6 * D))
            for l in range(L)]
    hp = _modulate(xp, mods[0][0], 1, 0, Bp * Tp)
    hs = _modulate(xs, mods[0][1], 1, 0, Ts)

    sw = _stacked_weights(dims, w_in, w_s, b_s, w_uq, w_ukv, w_pa, w_pb, w_pc, w_o, w_gate, w_up, w_down)
    new_k, new_v, new_ckv, new_kpe = [], [], [], []
    for l in range(L):
        w = _layer_weights(l, dims, sw)
        vecs = (sgu_ln_g[l], sgu_ln_b[l], q_norm_g[l], k_norm_g[l], mla_q_norm_g[l], mla_kv_norm_g[l],
                ln1_g[l], ln1_b[l], ln2_g[l], ln2_b[l])
        nxt_p = (mods[l + 1][0], 1, 0) if l + 1 < L else None
        nxt_s = (mods[l + 1][1], 1, 0) if l + 1 < L else None
        xp, hp, own = _trunk_layer(xp, hp, mods[l][0], w, vecs, dims, Bp, Tp, None, None, alpha, nxt_p)
        new_k.append(own[0].reshape(Bp, Tp, kvh, hd))
        new_v.append(own[1].reshape(Bp, Tp, kvh, hd))
        new_ckv.append(own[2].reshape(Bp, Tp, rank))
        new_kpe.append(own[3].reshape(Bp, Tp, rd))
        ctx = (cache_k[:, l], cache_v[:, l], cache_ckv[:, l], cache_kpe[:, l])
        xs, hs, _ = _trunk_layer(xs, hs, mods[l][1], w, vecs, dims, Bs, Ts, rope, ctx, alpha, nxt_s)

    return (xp.reshape(Bp, Tp, D), xs.reshape(Bs, Ts, D),
            jnp.stack(new_k, axis=1), jnp.stack(new_v, axis=1),
            jnp.stack(new_ckv, axis=1), jnp.stack(new_kpe, axis=1))
```

```python
import functools
from typing import NamedTuple

import jax
import jax.numpy as jnp
from jax import lax
from jax.experimental import pallas as pl
from jax.experimental.pallas import tpu as pltpu

F32 = jnp.float32
BF16 = jnp.bfloat16

NORM_EPS = 1e-6
ROPE_THETA = 10000.0
GRID_W = 64

LANES = 128
VMEM_LIMIT_BYTES = 56 * 1024 * 1024


def _cparams(n_axes):
    return pltpu.CompilerParams(
        dimension_semantics=("arbitrary",) * n_axes,
        vmem_limit_bytes=VMEM_LIMIT_BYTES,
    )


def _blk(n, pref):
    b = min(n, pref)
    while n % b:
        b //= 2
    return b


def _rms(a, g):
    ms = jnp.mean(a * a, axis=-1, keepdims=True)
    return a * lax.rsqrt(ms + NORM_EPS) * g


def _rope(y, cos, sin_signed, quarter):
    n = y.shape[-1]
    lane = lax.broadcasted_iota(jnp.int32, y.shape, 1)
    first = (lane & quarter) == 0
    rot = jnp.where(first, pltpu.roll(y, n - quarter, axis=1), pltpu.roll(y, quarter, axis=1))
    return y * cos + rot * sin_signed


def _sigmoid(x):
    return 0.5 * jnp.tanh(0.5 * x) + 0.5


def _layer_norm(z, g, b):
    mu = jnp.mean(z, axis=-1, keepdims=True)
    zc = z - mu
    var = jnp.mean(zc * zc, axis=-1, keepdims=True)
    return zc * lax.rsqrt(var + NORM_EPS) * g + b


def _ada_kernel(c_ref, w_ref, b_ref, o_ref):
    c = c_ref[...]
    s = (c * _sigmoid(c)).astype(BF16)
    o_ref[...] = jnp.dot(s, w_ref[...].astype(BF16), preferred_element_type=F32) + b_ref[...]


def _ada(cond, w_ada, b_ada):
    L, D, N = w_ada.shape
    R = cond.shape[0]
    bn = _blk(N, 512)
    return pl.pallas_call(
        _ada_kernel,
        grid=(L, N // bn),
        in_specs=[
            pl.BlockSpec((R, D), lambda l, j: (0, 0)),
            pl.BlockSpec((None, D, bn), lambda l, j: (l, 0, j)),
            pl.BlockSpec((None, 1, bn), lambda l, j: (l, 0, j)),
        ],
        out_specs=pl.BlockSpec((None, R, bn), lambda l, j: (l, 0, j)),
        out_shape=jax.ShapeDtypeStruct((L, R, N), F32),
        compiler_params=_cparams(2),
        name="ada_mod",
    )(cond, w_ada, b_ada.reshape(L, 1, N))


def _modulate_kernel(x_ref, sc_ref, sh_ref, h_ref):
    h_ref[...] = (x_ref[...] * (1.0 + sc_ref[...]) + sh_ref[...]).astype(h_ref.dtype)


def _mod_spec(D, k, rows_per_group, bt):
    return pl.BlockSpec((None, 1, D), lambda i, *_: ((i * bt) // rows_per_group, 0, k))


def _modulate(x, mods, k_sc, k_sh, rows_per_group):
    M, D = x.shape
    bt = _blk(rows_per_group, 512)
    return pl.pallas_call(
        _modulate_kernel,
        grid=(M // bt,),
        in_specs=[
            pl.BlockSpec((bt, D), lambda i: (i, 0)),
            _mod_spec(D, k_sc, rows_per_group, bt),
            _mod_spec(D, k_sh, rows_per_group, bt),
        ],
        out_specs=pl.BlockSpec((bt, D), lambda i: (i, 0)),
        out_shape=jax.ShapeDtypeStruct((M, D), BF16),
        compiler_params=_cparams(1),
        name="modulate",
    )(x, mods, mods)


def _ln_kernel(x_ref, y_ref, gate_ref, g_ref, b_ref, *rest, alpha, with_h):
    z = alpha * x_ref[...] + gate_ref[...] * y_ref[...].astype(F32)
    xn = _layer_norm(z, g_ref[...], b_ref[...])
    if with_h:
        sc_ref, sh_ref, xo_ref, h_ref = rest
        xo_ref[...] = xn
        h_ref[...] = (xn * (1.0 + sc_ref[...]) + sh_ref[...]).astype(h_ref.dtype)
    else:
        (xo_ref,) = rest
        xo_ref[...] = xn


def _ln_residual(x, y, mods, k_gate, ln_g, ln_b, alpha, rows_per_group, nxt=None):
    M, D = x.shape
    bt = _blk(rows_per_group, 256)
    row = pl.BlockSpec((bt, D), lambda i: (i, 0))
    vec = pl.BlockSpec((1, D), lambda i: (0, 0))
    in_specs = [row, row, _mod_spec(D, k_gate, rows_per_group, bt), vec, vec]
    args = [x, y, mods, ln_g.reshape(1, D), ln_b.reshape(1, D)]
    out_shape = [jax.ShapeDtypeStruct((M, D), F32)]
    out_specs = [row]
    if nxt is not None:
        mods_n, k_sc, k_sh = nxt
        in_specs += [_mod_spec(D, k_sc, rows_per_group, bt), _mod_spec(D, k_sh, rows_per_group, bt)]
        args += [mods_n, mods_n]
        out_shape.append(jax.ShapeDtypeStruct((M, D), BF16))
        out_specs.append(row)
    res = pl.pallas_call(
        functools.partial(_ln_kernel, alpha=alpha, with_h=nxt is not None),
        grid=(M // bt,),
        in_specs=in_specs,
        out_specs=out_specs,
        out_shape=out_shape,
        compiler_params=_cparams(1),
        name="ln_residual",
    )(*args)
    return res if nxt is not None else (res[0], None)


class _W(NamedTuple):
    arr: jax.Array
    layer: int
    col0: int
    n_cols: int
    t: bool = False


def _wfull(a):
    return _W(a[None], 0, 0, a.shape[1])


def _wspec(w, bn):
    l, ob = w.layer, w.col0 // bn
    assert w.col0 % bn == 0 and w.n_cols % bn == 0
    if w.t:
        return pl.BlockSpec((None, bn, w.arr.shape[2]), lambda i, j: (l, ob + j, 0))
    return pl.BlockSpec((None, w.arr.shape[1], bn), lambda i, j: (l, 0, ob + j))


def _mm_call(kernel, x, ws, bm, bn, extras, extra_specs, out_dtypes, name):
    M, K = x.shape
    n_cols = ws[0].n_cols
    in_specs = [pl.BlockSpec((bm, K), lambda i, j: (i, 0))]
    in_specs += [_wspec(w, bn) for w in ws]
    in_specs += list(extra_specs)
    out_shape = [jax.ShapeDtypeStruct((M, n_cols), dt) for dt in out_dtypes]
    out_specs = [pl.BlockSpec((bm, bn), lambda i, j: (i, j)) for _ in out_dtypes]
    return pl.pallas_call(
        kernel,
        grid=(M // bm, n_cols // bn),
        in_specs=in_specs,
        out_specs=out_specs,
        out_shape=out_shape,
        compiler_params=_cparams(2),
        name=name,
    )(x, *[w.arr for w in ws], *extras)


_NT = (((1,), (1,)), ((), ()))


def _xw(x_ref, w_ref, wt=False):
    w = w_ref[...].astype(BF16)
    if wt:
        return lax.dot_general(x_ref[...], w, _NT, preferred_element_type=F32)
    return jnp.dot(x_ref[...], w, preferred_element_type=F32)


def _mm_plain_kernel(x_ref, w_ref, o_ref, *, wt):
    o_ref[...] = _xw(x_ref, w_ref, wt).astype(o_ref.dtype)


def _mm_plain(x, w, out_dtype, name, bm=1024, bn=1024):
    bm, bn = _blk(x.shape[0], bm), _blk(w.n_cols, bn)
    kern = functools.partial(_mm_plain_kernel, wt=w.t)
    return _mm_call(kern, x, [w], bm, bn, [], [], [out_dtype], name)[0]


def _rope_specs(rope, bm, rows_per_batch):
    if rope is None:
        return [], []
    nb = rows_per_batch // bm
    spec = pl.BlockSpec((bm, LANES), lambda i, j: (i % nb, 0))
    return [rope[0], rope[1]], [spec, spec]


def _mm_heads_kernel(x_ref, w_ref, g_ref, *rest, n_norm, rope, quarter, wt):
    if rope:
        cos_ref, sin_ref, *outs = rest
    else:
        outs = rest
    acc = _xw(x_ref, w_ref, wt)
    for h in range(acc.shape[1] // LANES):
        a = acc[:, h * LANES:(h + 1) * LANES]
        if h < n_norm:
            a = _rms(a, g_ref[...])
            if rope:
                a = _rope(a, cos_ref[...], sin_ref[...], quarter)
        for o in outs:
            o[:, h * LANES:(h + 1) * LANES] = a.astype(o.dtype)


def _mm_heads(x, w, gain, rope, rows_per_batch, n_norm, out_dtypes, name, bm, bn):
    bm, bn = _blk(x.shape[0] if rope is None else rows_per_batch, bm), _blk(w.n_cols, bn)
    rargs, rspecs = _rope_specs(rope, bm, rows_per_batch)
    kern = functools.partial(_mm_heads_kernel, n_norm=n_norm, rope=rope is not None,
                             quarter=LANES // 4, wt=w.t)
    return _mm_call(kern, x, [w], bm, bn,
                    [gain.reshape(1, LANES)] + rargs,
                    [pl.BlockSpec((1, LANES), lambda i, j: (0, 0))] + rspecs,
                    out_dtypes, name)


def _mm_rms_kernel(x_ref, w_ref, g_ref, o_ref, *, wt):
    o_ref[...] = _rms(_xw(x_ref, w_ref, wt), g_ref[...]).astype(o_ref.dtype)


def _mm_rms(x, w, gain, out_dtype, name, bm=1024):
    N = w.n_cols
    bm = _blk(x.shape[0], bm)
    return _mm_call(functools.partial(_mm_rms_kernel, wt=w.t), x, [w], bm, N, [gain.reshape(1, N)],
                    [pl.BlockSpec((1, N), lambda i, j: (0, 0))], [out_dtype], name)[0]


def _mm_ckv_kernel(x_ref, w_ref, g_ref, *rest, rank, rope_dim, rope, wt):
    if rope:
        cos_ref, sin_ref, ckv_f, ckv_b, kpe_f, kpe_b = rest
    else:
        ckv_f, ckv_b, kpe_f, kpe_b = rest
    if wt:
        acc = lax.dot_general(x_ref[...], w_ref[:rank + LANES, :].astype(BF16), _NT, preferred_element_type=F32)
    else:
        acc = jnp.dot(x_ref[...], w_ref[:, :rank + LANES].astype(BF16), preferred_element_type=F32)
    ckv = _rms(acc[:, :rank], g_ref[...])
    kpe = acc[:, rank:rank + LANES]
    lane = lax.broadcasted_iota(jnp.int32, kpe.shape, 1)
    kpe = jnp.where(lane < rope_dim, kpe, 0.0)
    if rope:
        kpe = _rope(kpe, cos_ref[...], sin_ref[...], rope_dim // 4)
    ckv_f[...] = ckv
    ckv_b[...] = ckv.astype(BF16)
    kpe_f[...] = kpe
    kpe_b[...] = kpe.astype(BF16)


def _mm_ckv(x, w, gain, rope, rows_per_batch, rank, rope_dim, name, bm=1024):
    M, K = x.shape
    bw = w.n_cols
    assert bw >= rank + LANES and w.col0 % bw == 0
    bm = _blk(M if rope is None else rows_per_batch, bm)
    rargs, rspecs = _rope_specs(rope, bm, rows_per_batch)
    kern = functools.partial(_mm_ckv_kernel, rank=rank, rope_dim=rope_dim, rope=rope is not None, wt=w.t)
    in_specs = [pl.BlockSpec((bm, K), lambda i, j: (i, 0)),
                _wspec(w, bw),
                pl.BlockSpec((1, rank), lambda i, j: (0, 0))] + rspecs
    out_shape = [jax.ShapeDtypeStruct((M, rank), F32), jax.ShapeDtypeStruct((M, rank), BF16),
                 jax.ShapeDtypeStruct((M, LANES), F32), jax.ShapeDtypeStruct((M, LANES), BF16)]
    out_specs = [pl.BlockSpec((bm, rank), lambda i, j: (i, 0)), pl.BlockSpec((bm, rank), lambda i, j: (i, 0)),
                 pl.BlockSpec((bm, LANES), lambda i, j: (i, 0)), pl.BlockSpec((bm, LANES), lambda i, j: (i, 0))]
    return pl.pallas_call(
        kern, grid=(M // bm, 1), in_specs=in_specs, out_specs=out_specs, out_shape=out_shape,
        compiler_params=_cparams(2), name=name,
    )(x, w.arr, gain.reshape(1, rank), *rargs)


def _mm_uq_kernel(x_ref, w_ref, *rest, rope, quarter):
    if rope:
        cos_ref, sin_ref, o_ref = rest
    else:
        (o_ref,) = rest
    acc = _xw(x_ref, w_ref)
    for h in range(acc.shape[1] // LANES):
        a = acc[:, h * LANES:(h + 1) * LANES]
        if rope and h % 2 == 1:
            a = _rope(a, cos_ref[...], sin_ref[...], quarter)
        o_ref[:, h * LANES:(h + 1) * LANES] = a.astype(o_ref.dtype)


def _mm_uq(x, w, rope, rows_per_batch, rope_dim, name, bm=1024, bn=1024):
    bm, bn = _blk(x.shape[0] if rope is None else rows_per_batch, bm), _blk(w.n_cols, bn)
    rargs, rspecs = _rope_specs(rope, bm, rows_per_batch)
    kern = functools.partial(_mm_uq_kernel, rope=rope is not None, quarter=rope_dim // 4)
    return _mm_call(kern, x, [w], bm, bn, rargs, rspecs, [BF16], name)[0]


def _mm_sigmoid_kernel(x_ref, w_ref, o_ref, *, wt):
    o_ref[...] = _sigmoid(_xw(x_ref, w_ref, wt)).astype(o_ref.dtype)


def _mm_sigmoid(x, w, out_dtype, name, bm=1024, bn=1024):
    bm, bn = _blk(x.shape[0], bm), _blk(w.n_cols, bn)
    return _mm_call(functools.partial(_mm_sigmoid_kernel, wt=w.t), x, [w], bm, bn, [], [], [out_dtype], name)[0]


def _mm_swiglu_kernel(x_ref, wg_ref, wu_ref, o_ref):
    a = _xw(x_ref, wg_ref)
    b = _xw(x_ref, wu_ref)
    o_ref[...] = (a * _sigmoid(a) * b).astype(o_ref.dtype)


def _mm_swiglu(x, wg, wu, name, bm=1024, bn=256):
    bm, bn = _blk(x.shape[0], bm), _blk(wg.n_cols, bn)
    return _mm_call(_mm_swiglu_kernel, x, [wg, wu], bm, bn, [], [], [BF16], name)[0]


def _mm_merge_kernel(ya_ref, yb_ref, yc_ref, wa_ref, wb_ref, wc_ref, ga_ref, gb_ref, gc_ref, o_ref):
    a = jnp.dot(ya_ref[...], wa_ref[...], preferred_element_type=F32)
    b = jnp.dot(yb_ref[...], wb_ref[...], preferred_element_type=F32)
    c = jnp.dot(yc_ref[...], wc_ref[...], preferred_element_type=F32)
    ga, gb, gc = (g[...].astype(F32) for g in (ga_ref, gb_ref, gc_ref))
    o_ref[...] = (ga * a + gb * b + gc * c).astype(o_ref.dtype)


def _mm_merge(ya, yb, yc, wa, wb, wc, gates, name, bm=1024, bn=512):
    M = ya.shape[0]
    D = wa.n_cols
    bm, bn = _blk(M, bm), _blk(D, bn)
    nb = D // bn
    xs = lambda y: pl.BlockSpec((bm, y.shape[1]), lambda i, j: (i, 0))
    gsp = lambda t: pl.BlockSpec((bm, bn), lambda i, j: (i, j + t * nb))
    return pl.pallas_call(
        _mm_merge_kernel,
        grid=(M // bm, nb),
        in_specs=[xs(ya), xs(yb), xs(yc), _wspec(wa, bn), _wspec(wb, bn), _wspec(wc, bn),
                  gsp(0), gsp(1), gsp(2)],
        out_specs=pl.BlockSpec((bm, bn), lambda i, j: (i, j)),
        out_shape=jax.ShapeDtypeStruct((M, D), BF16),
        compiler_params=_cparams(2),
        name=name,
    )(ya, yb, yc, wa.arr, wb.arr, wc.arr, gates, gates, gates)


def _sgu_kernel(u_ref, v_ref, lg_ref, lb_ref, ws_ref, bias_ref, o_ref, *, chunk, groups):
    vn = _layer_norm(v_ref[...], lg_ref[...], lb_ref[...]).astype(BF16)
    gd = vn.shape[1] // groups
    for c in range(vn.shape[0] // chunk):
        r0 = c * chunk
        for g in range(groups):
            c0 = g * gd
            s = jnp.dot(ws_ref[g], vn[r0:r0 + chunk, c0:c0 + gd], preferred_element_type=F32)
            s = s + bias_ref[:, c0:c0 + gd]
            o_ref[r0:r0 + chunk, c0:c0 + gd] = (u_ref[r0:r0 + chunk, c0:c0 + gd] * s).astype(o_ref.dtype)


def _sgu(uv, ln_g, ln_b, w_s, bias_full, name):
    M = uv.shape[0]
    A = uv.shape[1] // 2
    G, C, _ = w_s.shape
    bt = _blk(M, 4 * C)
    return pl.pallas_call(
        functools.partial(_sgu_kernel, chunk=C, groups=G),
        grid=(M // bt,),
        in_specs=[
            pl.BlockSpec((bt, A), lambda i: (i, 0)),
            pl.BlockSpec((bt, A), lambda i: (i, 1)),
            pl.BlockSpec((1, A), lambda i: (0, 0)),
            pl.BlockSpec((1, A), lambda i: (0, 0)),
            pl.BlockSpec((G, C, C), lambda i: (0, 0, 0)),
            pl.BlockSpec((C, A), lambda i: (0, 0)),
        ],
        out_specs=pl.BlockSpec((bt, A), lambda i: (i, 0)),
        out_shape=jax.ShapeDtypeStruct((M, A), BF16),
        compiler_params=_cparams(1),
        name=name,
    )(uv, uv, ln_g.reshape(1, A), ln_b.reshape(1, A), w_s, bias_full)


_LOG2E = 1.4426950408889634


def _attend(q, k, v, scale):
    s = lax.dot_general(q, k, _NT, preferred_element_type=F32)
    m = jnp.max(s, axis=-1, keepdims=True)
    p = jnp.exp2((s - m) * (scale * _LOG2E))
    l = jnp.sum(p, axis=-1, keepdims=True)
    return jnp.dot(p.astype(BF16), v, preferred_element_type=F32) / l


def _chunks(n_heads, rows, sub):
    out = [(h, r, sub) for h in range(n_heads) for r in range(0, rows, sub)]
    if sub >= 512 and len(out) >= 4:
        h, r, _ = out[0]
        out[0:1] = [(h, r, sub // 4), (h, r + sub // 4, sub - sub // 4)]
        h, r, _ = out[-1]
        out[-1:] = [(h, r, sub - sub // 4), (h, r + sub - sub // 4, sub // 4)]
    return out


def _gqa_kernel(q_ref, g_ref, *rest, nkv, group, hd, scale, sub, rope):
    if rope:
        cos_ref, sin_ref, k_ref, v_ref, o_ref = rest
    else:
        k_ref, v_ref, o_ref = rest
    for h, r, n in _chunks(nkv * group, q_ref.shape[0], sub):
        kv = h // group
        q = _rms(q_ref[r:r + n, h * hd:(h + 1) * hd], g_ref[...])
        if rope:
            q = _rope(q, cos_ref[r:r + n, :], sin_ref[r:r + n, :], hd // 4)
        o = _attend(q.astype(BF16), k_ref[:, kv * hd:(kv + 1) * hd],
                    v_ref[:, kv * hd:(kv + 1) * hd], scale)
        o_ref[r:r + n, h * hd:(h + 1) * hd] = o.astype(o_ref.dtype)


def _gqa(q, gain, rope, k, v, kv_heads, hd, name, bq, nkv, sub=512):
    B, Nq, W = q.shape
    S = k.shape[1]
    group = W // (kv_heads * hd)
    bq = _blk(Nq, bq)
    gw = nkv * group * hd
    sub = _blk(bq, sub)
    tab = pl.BlockSpec((bq, hd), lambda b, n, i: (i, 0))
    return pl.pallas_call(
        functools.partial(_gqa_kernel, nkv=nkv, group=group, hd=hd, scale=hd ** -0.5, sub=sub,
                          rope=rope is not None),
        grid=(B, kv_heads // nkv, Nq // bq),
        in_specs=[
            pl.BlockSpec((None, bq, gw), lambda b, n, i: (b, i, n)),
            pl.BlockSpec((1, hd), lambda b, n, i: (0, 0)),
        ] + ([tab, tab] if rope is not None else []) + [
            pl.BlockSpec((None, S, nkv * hd), lambda b, n, i: (b, 0, n)),
            pl.BlockSpec((None, S, nkv * hd), lambda b, n, i: (b, 0, n)),
        ],
        out_specs=pl.BlockSpec((None, bq, gw), lambda b, n, i: (b, i, n)),
        out_shape=jax.ShapeDtypeStruct((B, Nq, W), BF16),
        compiler_params=_cparams(3),
        name=name,
    )(q, gain.reshape(1, hd), *(rope if rope is not None else ()), k, v)


def _mla_kernel(q_ref, kn_ref, kpe_ref, v_ref, o_ref, *, hb, scale, sub):
    kpe = kpe_ref[...]
    ks = [jnp.concatenate([kn_ref[:, h * LANES:(h + 1) * LANES], kpe], axis=1) for h in range(hb)]
    for h, r, n in _chunks(hb, q_ref.shape[0], sub):
        o = _attend(q_ref[r:r + n, 2 * h * LANES:2 * (h + 1) * LANES], ks[h],
                    v_ref[:, h * LANES:(h + 1) * LANES], scale)
        o_ref[r:r + n, h * LANES:(h + 1) * LANES] = o.astype(o_ref.dtype)


def _mla(q, kv, kpe, heads, scale, name, bq, hb, sub=512):
    B, Nq, _ = q.shape
    S = kv.shape[1]
    bq = _blk(Nq, bq)
    sub = _blk(bq, sub)
    nhb = heads // hb
    return pl.pallas_call(
        functools.partial(_mla_kernel, hb=hb, scale=scale, sub=sub),
        grid=(B, nhb, Nq // bq),
        in_specs=[
            pl.BlockSpec((None, bq, 2 * hb * LANES), lambda b, h, i: (b, i, h)),
            pl.BlockSpec((None, S, hb * LANES), lambda b, h, i: (b, 0, h)),
            pl.BlockSpec((None, S, LANES), lambda b, h, i: (b, 0, 0)),
            pl.BlockSpec((None, S, hb * LANES), lambda b, h, i: (b, 0, nhb + h)),
        ],
        out_specs=pl.BlockSpec((None, bq, hb * LANES), lambda b, h, i: (b, i, h)),
        out_shape=jax.ShapeDtypeStruct((B, Nq, heads * LANES), BF16),
        compiler_params=_cparams(3),
        name=name,
    )(q, kv, kpe, kv)


def _rope_tables(n, dim):
    rows = n // GRID_W
    row = jnp.repeat(jnp.arange(rows, dtype=F32), GRID_W)
    col = jnp.tile(jnp.arange(GRID_W, dtype=F32), rows)
    quarter = dim // 4
    freqs = ROPE_THETA ** (-jnp.arange(quarter, dtype=F32) / quarter)
    ra = row[:, None] * freqs[None, :]
    ca = col[:, None] * freqs[None, :]
    ang = jnp.concatenate([ra, ra, ca, ca], axis=-1)
    sign = jnp.where((jnp.arange(dim) // quarter) % 2 == 0, -1.0, 1.0).astype(F32)
    cos = jnp.cos(ang)
    sin = jnp.sin(ang) * sign[None, :]
    if dim < LANES:
        cos = jnp.pad(cos, ((0, 0), (0, LANES - dim)), constant_values=1.0)
        sin = jnp.pad(sin, ((0, 0), (0, LANES - dim)))
    return cos, sin


def _stacked_weights(dims, w_in, w_s, b_s, w_uq, w_ukv, w_pa, w_pb, w_pc, w_o, w_gate, w_up, w_down):
    A, qw, kvw, qr, rank, rd, D, mh, nope, vd = dims
    L = w_in.shape[0]
    bf = lambda a: a.astype(BF16)
    o_c = 2 * A + qw + 2 * kvw + qr
    o_g = o_c + rank + rd
    s = {}
    s['in'] = bf(jnp.swapaxes(w_in, 1, 2))
    s['o_c'] = o_c
    s['g'] = s['in'][:, o_g:]
    uq = w_uq.reshape(L, qr, mh, nope + rd)
    s['uq'] = bf(jnp.pad(uq, ((0, 0), (0, 0), (0, 0), (0, 2 * LANES - nope - rd))).reshape(L, qr, mh * 2 * LANES))
    ukv = w_ukv.reshape(L, rank, mh, nope + vd)
    s['ukv'] = bf(jnp.concatenate([ukv[..., :nope].reshape(L, rank, mh * nope),
                                   ukv[..., nope:].reshape(L, rank, mh * vd)], axis=-1))
    for name, a in (('pa', w_pa), ('pb', w_pb), ('pc', w_pc), ('o', w_o), ('down', w_down), ('s', w_s)):
        s[name] = bf(a)
    s['gate'], s['up'] = w_gate, w_up
    gd = A // w_s.shape[1]
    s['sb'] = jnp.repeat(jnp.swapaxes(b_s, 1, 2), gd, axis=2)
    return s


def _layer_weights(l, dims, s):
    A, qw, kvw, qr, rank, rd, D, mh, nope, vd = dims
    o = [0, 2 * A, 2 * A + qw, 2 * A + qw + 2 * kvw, 2 * A + qw + 2 * kvw + qr]
    full = lambda a: _W(a, l, 0, a.shape[2])
    w = {}
    w['uv'] = _W(s['in'], l, o[0], 2 * A, True)
    w['q'] = _W(s['in'], l, o[1], qw, True)
    w['kv'] = _W(s['in'], l, o[2], 2 * kvw, True)
    w['cq'] = _W(s['in'], l, o[3], qr, True)
    w['ckv'] = _W(s['in'], l, s['o_c'], 2 * rank, True)
    w['g'] = _W(s['g'], l, 0, s['g'].shape[1], True)
    for name in ('uq', 'ukv', 'pa', 'pb', 'pc', 'o', 'gate', 'up', 'down'):
        w[name] = full(s[name])
    w['s'] = s['s'][l]
    w['sb'] = s['sb'][l]
    return w


def _trunk_layer(x, h, mods, w, vecs, dims, B, T, rope, ctx, alpha, nxt):
    A, qw, kvw, qr, rank, rd, D, mh, nope, vd = dims
    sgu_g, sgu_b, qg, kg, cqg, ckvg, ln1g, ln1b, ln2g, ln2b = vecs
    M = B * T
    G = mods.shape[0]
    rpg = M // G
    hd = LANES
    kvh = kvw // hd
    tag = "s" if rope is not None else "p"
    rope_g = None if rope is None else rope[0]
    rope_m = None if rope is None else rope[1]

    uv = _mm_plain(h, w['uv'], F32, "proj_uv_" + tag)
    q = _mm_plain(h, w['q'], F32, "proj_q_" + tag)
    kv_f, kv_b = _mm_heads(h, w['kv'], kg, rope_g, T, kvh, [F32, BF16], "proj_kv_" + tag,
                           bm=1024, bn=2 * kvw)
    cq = _mm_rms(h, w['cq'], cqg, BF16, "proj_cq_" + tag)
    ckv_f, ckv_b, kpe_f, kpe_b = _mm_ckv(h, w['ckv'], ckvg, rope_m, T, rank, rd, "proj_ckv_" + tag)
    gates = _mm_sigmoid(h, w['g'], BF16, "proj_gates_" + tag)

    y_a = _sgu(uv, sgu_g, sgu_b, w['s'], w['sb'], "sgu_" + tag)

    k_b = kv_b[:, :kvw].reshape(B, T, kvw)
    v_b = kv_b[:, kvw:].reshape(B, T, kvw)
    if ctx is not None:
        c_k, c_v, c_ckv, c_kpe = ctx
        P = c_k.shape[1]
        k_b = jnp.concatenate([c_k.reshape(B, P, kvw).astype(BF16), k_b], axis=1)
        v_b = jnp.concatenate([c_v.reshape(B, P, kvw).astype(BF16), v_b], axis=1)
    long_keys = ctx is not None
    y_b = _gqa(q.reshape(B, T, qw), qg, rope_g, k_b, v_b, kvh, hd, "gqa_" + tag,
               bq=1024 if long_keys else 256, nkv=1 if long_keys else kvh).reshape(M, qw)

    qc = _mm_uq(cq, w['uq'], rope_m, T, rd, "mla_uq_" + tag, bn=2048)
    ckv_all = ckv_b.reshape(B, T, rank)
    kpe_all = kpe_b.reshape(B, T, LANES)
    if ctx is not None:
        ckv_all = jnp.concatenate([c_ckv.astype(BF16), ckv_all], axis=1)
        c_kpe_pad = jnp.pad(c_kpe, ((0, 0), (0, 0), (0, LANES - rd))).astype(BF16)
        kpe_all = jnp.concatenate([c_kpe_pad, kpe_all], axis=1)
    S = ckv_all.shape[1]
    kvu = _mm_plain(ckv_all.reshape(B * S, rank), w['ukv'], BF16, "mla_ukv_" + tag, bn=4096)
    y_c = _mla(qc.reshape(B, T, mh * 2 * LANES), kvu.reshape(B, S, mh * (nope + vd)), kpe_all,
               mh, (nope + rd) ** -0.5, "mla_" + tag,
               bq=4096 if long_keys else 256, hb=1 if long_keys else mh).reshape(M, mh * vd)

    merged = _mm_merge(y_a, y_b, y_c, w['pa'], w['pb'], w['pc'], gates, "merge_" + tag)
    mix = _mm_plain(merged, w['o'], BF16, "proj_o_" + tag)
    x1, h2 = _ln_residual(x, mix, mods, 2, ln1g, ln1b, alpha, rpg, nxt=(mods, 4, 3))
    hid = _mm_swiglu(h2, w['gate'], w['up'], "ffn_in_" + tag)
    ff = _mm_plain(hid, w['down'], BF16, "ffn_out_" + tag, bm=512, bn=512)
    x2, h_next = _ln_residual(x1, ff, mods, 5, ln2g, ln2b, alpha, rpg, nxt=nxt)

    own = (kv_f[:, :kvw], kv_f[:, kvw:], ckv_f, kpe_f[:, :rd])
    return x2, h_next, own


def kernel(x_prompt, x_sample, cache_k, cache_v, cache_ckv, cache_kpe, c, c_ctx,
           w_ada, b_ada, w_in, sgu_ln_g, sgu_ln_b, w_s, b_s, q_norm_g, k_norm_g,
           mla_q_norm_g, mla_kv_norm_g, w_uq, w_ukv, w_pa, w_pb, w_pc, w_o,
           ln1_g, ln1_b, ln2_g, ln2_b, w_gate, w_up, w_down):
    Bp, Tp, D = x_prompt.shape
    Bs, Ts, _ = x_sample.shape
    L = w_ada.shape[0]
    A = sgu_ln_g.shape[1]
    hd = q_norm_g.shape[1]
    kvh = cache_k.shape[3]
    qw = w_pb.shape[1]
    kvw = kvh * hd
    qr = mla_q_norm_g.shape[1]
    rank = mla_kv_norm_g.shape[1]
    rd = cache_kpe.shape[-1]
    vd = LANES
    mh = w_pc.shape[1] // vd
    nope = w_uq.shape[2] // mh - rd
    assert hd == LANES and nope == LANES and w_ukv.shape[2] == mh * (nope + vd)
    dims = (A, qw, kvw, qr, rank, rd, D, mh, nope, vd)
    alpha = float((2 * L) ** 0.25)

    R = -(-(1 + Bs) // 8) * 8
    cond = jnp.concatenate([c_ctx[None, :], c, jnp.zeros((R - 1 - Bs, D), F32)], axis=0)
    mods_all = _ada(cond, w_ada, b_ada)

    rope = (_rope_tables(Ts, hd), _rope_tables(Ts, rd))

    xp = x_prompt.reshape(Bp * Tp, D)
    xs = x_sample.reshape(Bs * Ts, D)
    mods = [(mods_all[l, 0:1].reshape(1, 1, 6 * D), mods_all[l, 1:1 + Bs].reshape(Bs, 1, 6 * D))
            for l in range(L)]
    hp = _modulate(xp, mods[0][0], 1, 0, Bp * Tp)
    hs = _modulate(xs, mods[0][1], 1, 0, Ts)

    sw = _stacked_weights(dims, w_in, w_s, b_s, w_uq, w_ukv, w_pa, w_pb, w_pc, w_o, w_gate, w_up, w_down)
    new_k, new_v, new_ckv, new_kpe = [], [], [], []
    for l in range(L):
        w = _layer_weights(l, dims, sw)
        vecs = (sgu_ln_g[l], sgu_ln_b[l], q_norm_g[l], k_norm_g[l], mla_q_norm_g[l], mla_kv_norm_g[l],
                ln1_g[l], ln1_b[l], ln2_g[l], ln2_b[l])
        nxt_p = (mods[l + 1][0], 1, 0) if l + 1 < L else None
        nxt_s = (mods[l + 1][1], 1, 0) if l + 1 < L else None
        xp, hp, own = _trunk_layer(xp, hp, mods[l][0], w, vecs, dims, Bp, Tp, None, None, alpha, nxt_p)
        new_k.append(own[0].reshape(Bp, Tp, kvh, hd))
        new_v.append(own[1].reshape(Bp, Tp, kvh, hd))
        new_ckv.append(own[2].reshape(Bp, Tp, rank))
        new_kpe.append(own[3].reshape(Bp, Tp, rd))
        ctx = (cache_k[:, l], cache_v[:, l], cache_ckv[:, l], cache_kpe[:, l])
        xs, hs, _ = _trunk_layer(xs, hs, mods[l][1], w, vecs, dims, Bs, Ts, rope, ctx, alpha, nxt_s)

    return (xp.reshape(Bp, Tp, D), xs.reshape(Bs, Ts, D),
            jnp.stack(new_k, axis=1), jnp.stack(new_v, axis=1),
            jnp.stack(new_ckv, axis=1), jnp.stack(new_kpe, axis=1))
```

```python
import functools
from typing import NamedTuple

import jax
import jax.numpy as jnp
from jax import lax
from jax.experimental import pallas as pl
from jax.experimental.pallas import tpu as pltpu

F32 = jnp.float32
BF16 = jnp.bfloat16

NORM_EPS = 1e-6
ROPE_THETA = 10000.0
GRID_W = 64

LANES = 128
VMEM_LIMIT_BYTES = 56 * 1024 * 1024


def _cparams(n_axes):
    return pltpu.CompilerParams(
        dimension_semantics=("arbitrary",) * n_axes,
        vmem_limit_bytes=VMEM_LIMIT_BYTES,
    )


def _blk(n, pref):
    b = min(n, pref)
    while n % b:
        b //= 2
    return b


def _rms(a, g):
    ms = jnp.mean(a * a, axis=-1, keepdims=True)
    return a * lax.rsqrt(ms + NORM_EPS) * g


def _rope(y, cos, sin_signed, quarter):
    n = y.shape[-1]
    lane = lax.broadcasted_iota(jnp.int32, y.shape, 1)
    first = (lane & quarter) == 0
    rot = jnp.where(first, pltpu.roll(y, n - quarter, axis=1), pltpu.roll(y, quarter, axis=1))
    return y * cos + rot * sin_signed


def _sigmoid(x):
    return 0.5 * jnp.tanh(0.5 * x) + 0.5


def _layer_norm(z, g, b):
    mu = jnp.mean(z, axis=-1, keepdims=True)
    zc = z - mu
    var = jnp.mean(zc * zc, axis=-1, keepdims=True)
    return zc * lax.rsqrt(var + NORM_EPS) * g + b


def _ada_kernel(c_ref, w_ref, b_ref, o_ref):
    c = c_ref[...]
    s = (c * _sigmoid(c)).astype(BF16)
    o_ref[...] = jnp.dot(s, w_ref[...].astype(BF16), preferred_element_type=F32) + b_ref[...]


def _ada(cond, w_ada, b_ada):
    L, D, N = w_ada.shape
    R = cond.shape[0]
    bn = _blk(N, 512)
    return pl.pallas_call(
        _ada_kernel,
        grid=(L, N // bn),
        in_specs=[
            pl.BlockSpec((R, D), lambda l, j: (0, 0)),
            pl.BlockSpec((None, D, bn), lambda l, j: (l, 0, j)),
            pl.BlockSpec((None, 1, bn), lambda l, j: (l, 0, j)),
        ],
        out_specs=pl.BlockSpec((None, R, bn), lambda l, j: (l, 0, j)),
        out_shape=jax.ShapeDtypeStruct((L, R, N), F32),
        compiler_params=_cparams(2),
        name="ada_mod",
    )(cond, w_ada, b_ada.reshape(L, 1, N))


def _modulate_kernel(x_ref, sc_ref, sh_ref, h_ref):
    h_ref[...] = (x_ref[...] * (1.0 + sc_ref[...]) + sh_ref[...]).astype(h_ref.dtype)


def _mod_spec(D, k, rows_per_group, bt):
    return pl.BlockSpec((None, 1, D), lambda i, *_: ((i * bt) // rows_per_group, 0, k))


def _modulate(x, mods, k_sc, k_sh, rows_per_group):
    M, D = x.shape
    bt = _blk(rows_per_group, 512)
    return pl.pallas_call(
        _modulate_kernel,
        grid=(M // bt,),
        in_specs=[
            pl.BlockSpec((bt, D), lambda i: (i, 0)),
            _mod_spec(D, k_sc, rows_per_group, bt),
            _mod_spec(D, k_sh, rows_per_group, bt),
        ],
        out_specs=pl.BlockSpec((bt, D), lambda i: (i, 0)),
        out_shape=jax.ShapeDtypeStruct((M, D), BF16),
        compiler_params=_cparams(1),
        name="modulate",
    )(x, mods, mods)


def _ln_kernel(x_ref, y_ref, gate_ref, g_ref, b_ref, *rest, alpha, with_h):
    z = alpha * x_ref[...] + gate_ref[...] * y_ref[...].astype(F32)
    xn = _layer_norm(z, g_ref[...], b_ref[...])
    if with_h:
        sc_ref, sh_ref, xo_ref, h_ref = rest
        xo_ref[...] = xn
        h_ref[...] = (xn * (1.0 + sc_ref[...]) + sh_ref[...]).astype(h_ref.dtype)
    else:
        (xo_ref,) = rest
        xo_ref[...] = xn


def _ln_residual(x, y, mods, k_gate, ln_g, ln_b, alpha, rows_per_group, nxt=None):
    M, D = x.shape
    bt = _blk(rows_per_group, 256)
    row = pl.BlockSpec((bt, D), lambda i: (i, 0))
    vec = pl.BlockSpec((1, D), lambda i: (0, 0))
    in_specs = [row, row, _mod_spec(D, k_gate, rows_per_group, bt), vec, vec]
    args = [x, y, mods, ln_g.reshape(1, D), ln_b.reshape(1, D)]
    out_shape = [jax.ShapeDtypeStruct((M, D), F32)]
    out_specs = [row]
    if nxt is not None:
        mods_n, k_sc, k_sh = nxt
        in_specs += [_mod_spec(D, k_sc, rows_per_group, bt), _mod_spec(D, k_sh, rows_per_group, bt)]
        args += [mods_n, mods_n]
        out_shape.append(jax.ShapeDtypeStruct((M, D), BF16))
        out_specs.append(row)
    res = pl.pallas_call(
        functools.partial(_ln_kernel, alpha=alpha, with_h=nxt is not None),
        grid=(M // bt,),
        in_specs=in_specs,
        out_specs=out_specs,
        out_shape=out_shape,
        compiler_params=_cparams(1),
        name="ln_residual",
    )(*args)
    return res if nxt is not None else (res[0], None)


class _W(NamedTuple):
    arr: jax.Array
    layer: int
    col0: int
    n_cols: int
    t: bool = False


def _wfull(a):
    return _W(a[None], 0, 0, a.shape[1])


def _wspec(w, bn):
    l, ob = w.layer, w.col0 // bn
    assert w.col0 % bn == 0 and w.n_cols % bn == 0
    if w.t:
        return pl.BlockSpec((None, bn, w.arr.shape[2]), lambda i, j: (l, ob + j, 0))
    return pl.BlockSpec((None, w.arr.shape[1], bn), lambda i, j: (l, 0, ob + j))


def _mm_call(kernel, x, ws, bm, bn, extras, extra_specs, out_dtypes, name):
    M, K = x.shape
    n_cols = ws[0].n_cols
    in_specs = [pl.BlockSpec((bm, K), lambda i, j: (i, 0))]
    in_specs += [_wspec(w, bn) for w in ws]
    in_specs += list(extra_specs)
    out_shape = [jax.ShapeDtypeStruct((M, n_cols), dt) for dt in out_dtypes]
    out_specs = [pl.BlockSpec((bm, bn), lambda i, j: (i, j)) for _ in out_dtypes]
    return pl.pallas_call(
        kernel,
        grid=(M // bm, n_cols // bn),
        in_specs=in_specs,
        out_specs=out_specs,
        out_shape=out_shape,
        compiler_params=_cparams(2),
        name=name,
    )(x, *[w.arr for w in ws], *extras)


_NT = (((1,), (1,)), ((), ()))


def _xw(x_ref, w_ref, wt=False):
    w = w_ref[...].astype(BF16)
    if wt:
        return lax.dot_general(x_ref[...], w, _NT, preferred_element_type=F32)
    return jnp.dot(x_ref[...], w, preferred_element_type=F32)


def _mm_plain_kernel(x_ref, w_ref, o_ref, *, wt):
    o_ref[...] = _xw(x_ref, w_ref, wt).astype(o_ref.dtype)


def _mm_plain(x, w, out_dtype, name, bm=1024, bn=1024):
    bm, bn = _blk(x.shape[0], bm), _blk(w.n_cols, bn)
    kern = functools.partial(_mm_plain_kernel, wt=w.t)
    return _mm_call(kern, x, [w], bm, bn, [], [], [out_dtype], name)[0]


def _rope_specs(rope, bm, rows_per_batch):
    if rope is None:
        return [], []
    nb = rows_per_batch // bm
    spec = pl.BlockSpec((bm, LANES), lambda i, j: (i % nb, 0))
    return [rope[0], rope[1]], [spec, spec]


def _mm_heads_kernel(x_ref, w_ref, g_ref, *rest, n_norm, rope, quarter, wt):
    if rope:
        cos_ref, sin_ref, *outs = rest
    else:
        outs = rest
    acc = _xw(x_ref, w_ref, wt)
    for h in range(acc.shape[1] // LANES):
        a = acc[:, h * LANES:(h + 1) * LANES]
        if h < n_norm:
            a = _rms(a, g_ref[...])
            if rope:
                a = _rope(a, cos_ref[...], sin_ref[...], quarter)
        for o in outs:
            o[:, h * LANES:(h + 1) * LANES] = a.astype(o.dtype)


def _mm_heads(x, w, gain, rope, rows_per_batch, n_norm, out_dtypes, name, bm, bn):
    bm, bn = _blk(x.shape[0] if rope is None else rows_per_batch, bm), _blk(w.n_cols, bn)
    rargs, rspecs = _rope_specs(rope, bm, rows_per_batch)
    kern = functools.partial(_mm_heads_kernel, n_norm=n_norm, rope=rope is not None,
                             quarter=LANES // 4, wt=w.t)
    return _mm_call(kern, x, [w], bm, bn,
                    [gain.reshape(1, LANES)] + rargs,
                    [pl.BlockSpec((1, LANES), lambda i, j: (0, 0))] + rspecs,
                    out_dtypes, name)


def _mm_rms_kernel(x_ref, w_ref, g_ref, o_ref, *, wt):
    o_ref[...] = _rms(_xw(x_ref, w_ref, wt), g_ref[...]).astype(o_ref.dtype)


def _mm_rms(x, w, gain, out_dtype, name, bm=1024):
    N = w.n_cols
    bm = _blk(x.shape[0], bm)
    return _mm_call(functools.partial(_mm_rms_kernel, wt=w.t), x, [w], bm, N, [gain.reshape(1, N)],
                    [pl.BlockSpec((1, N), lambda i, j: (0, 0))], [out_dtype], name)[0]


def _mm_ckv_kernel(x_ref, w_ref, g_ref, *rest, rank, rope_dim, rope, wt):
    if rope:
        cos_ref, sin_ref, ckv_f, ckv_b, kpe_f, kpe_b = rest
    else:
        ckv_f, ckv_b, kpe_f, kpe_b = rest
    if wt:
        acc = lax.dot_general(x_ref[...], w_ref[:rank + LANES, :].astype(BF16), _NT, preferred_element_type=F32)
    else:
        acc = jnp.dot(x_ref[...], w_ref[:, :rank + LANES].astype(BF16), preferred_element_type=F32)
    ckv = _rms(acc[:, :rank], g_ref[...])
    kpe = acc[:, rank:rank + LANES]
    lane = lax.broadcasted_iota(jnp.int32, kpe.shape, 1)
    kpe = jnp.where(lane < rope_dim, kpe, 0.0)
    if rope:
        kpe = _rope(kpe, cos_ref[...], sin_ref[...], rope_dim // 4)
    ckv_f[...] = ckv
    ckv_b[...] = ckv.astype(BF16)
    kpe_f[...] = kpe
    kpe_b[...] = kpe.astype(BF16)


def _mm_ckv(x, w, gain, rope, rows_per_batch, rank, rope_dim, name, bm=1024):
    M, K = x.shape
    bw = w.n_cols
    assert bw >= rank + LANES and w.col0 % bw == 0
    bm = _blk(M if rope is None else rows_per_batch, bm)
    rargs, rspecs = _rope_specs(rope, bm, rows_per_batch)
    kern = functools.partial(_mm_ckv_kernel, rank=rank, rope_dim=rope_dim, rope=rope is not None, wt=w.t)
    in_specs = [pl.BlockSpec((bm, K), lambda i, j: (i, 0)),
                _wspec(w, bw),
                pl.BlockSpec((1, rank), lambda i, j: (0, 0))] + rspecs
    out_shape = [jax.ShapeDtypeStruct((M, rank), F32), jax.ShapeDtypeStruct((M, rank), BF16),
                 jax.ShapeDtypeStruct((M, LANES), F32), jax.ShapeDtypeStruct((M, LANES), BF16)]
    out_specs = [pl.BlockSpec((bm, rank), lambda i, j: (i, 0)), pl.BlockSpec((bm, rank), lambda i, j: (i, 0)),
                 pl.BlockSpec((bm, LANES), lambda i, j: (i, 0)), pl.BlockSpec((bm, LANES), lambda i, j: (i, 0))]
    return pl.pallas_call(
        kern, grid=(M // bm, 1), in_specs=in_specs, out_specs=out_specs, out_shape=out_shape,
        compiler_params=_cparams(2), name=name,
    )(x, w.arr, gain.reshape(1, rank), *rargs)


def _mm_uq_kernel(x_ref, w_ref, *rest, rope, quarter):
    if rope:
        cos_ref, sin_ref, o_ref = rest
    else:
        (o_ref,) = rest
    acc = _xw(x_ref, w_ref)
    for h in range(acc.shape[1] // LANES):
        a = acc[:, h * LANES:(h + 1) * LANES]
        if rope and h % 2 == 1:
            a = _rope(a, cos_ref[...], sin_ref[...], quarter)
        o_ref[:, h * LANES:(h + 1) * LANES] = a.astype(o_ref.dtype)


def _mm_uq(x, w, rope, rows_per_batch, rope_dim, name, bm=1024, bn=1024):
    bm, bn = _blk(x.shape[0] if rope is None else rows_per_batch, bm), _blk(w.n_cols, bn)
    rargs, rspecs = _rope_specs(rope, bm, rows_per_batch)
    kern = functools.partial(_mm_uq_kernel, rope=rope is not None, quarter=rope_dim // 4)
    return _mm_call(kern, x, [w], bm, bn, rargs, rspecs, [BF16], name)[0]


def _mm_sigmoid_kernel(x_ref, w_ref, o_ref, *, wt):
    o_ref[...] = _sigmoid(_xw(x_ref, w_ref, wt)).astype(o_ref.dtype)


def _mm_sigmoid(x, w, out_dtype, name, bm=1024, bn=1024):
    bm, bn = _blk(x.shape[0], bm), _blk(w.n_cols, bn)
    return _mm_call(functools.partial(_mm_sigmoid_kernel, wt=w.t), x, [w], bm, bn, [], [], [out_dtype], name)[0]


def _mm_swiglu_kernel(x_ref, wg_ref, wu_ref, o_ref):
    a = _xw(x_ref, wg_ref)
    b = _xw(x_ref, wu_ref)
    o_ref[...] = (a * _sigmoid(a) * b).astype(o_ref.dtype)


def _mm_swiglu(x, wg, wu, name, bm=1024, bn=256):
    bm, bn = _blk(x.shape[0], bm), _blk(wg.n_cols, bn)
    return _mm_call(_mm_swiglu_kernel, x, [wg, wu], bm, bn, [], [], [BF16], name)[0]


def _mm_merge_kernel(ya_ref, yb_ref, yc_ref, wa_ref, wb_ref, wc_ref, ga_ref, gb_ref, gc_ref, o_ref):
    a = jnp.dot(ya_ref[...], wa_ref[...], preferred_element_type=F32)
    b = jnp.dot(yb_ref[...], wb_ref[...], preferred_element_type=F32)
    c = jnp.dot(yc_ref[...], wc_ref[...], preferred_element_type=F32)
    ga, gb, gc = (g[...].astype(F32) for g in (ga_ref, gb_ref, gc_ref))
    o_ref[...] = (ga * a + gb * b + gc * c).astype(o_ref.dtype)


def _mm_merge(ya, yb, yc, wa, wb, wc, gates, name, bm=1024, bn=512):
    M = ya.shape[0]
    D = wa.n_cols
    bm, bn = _blk(M, bm), _blk(D, bn)
    nb = D // bn
    xs = lambda y: pl.BlockSpec((bm, y.shape[1]), lambda i, j: (i, 0))
    gsp = lambda t: pl.BlockSpec((bm, bn), lambda i, j: (i, j + t * nb))
    return pl.pallas_call(
        _mm_merge_kernel,
        grid=(M // bm, nb),
        in_specs=[xs(ya), xs(yb), xs(yc), _wspec(wa, bn), _wspec(wb, bn), _wspec(wc, bn),
                  gsp(0), gsp(1), gsp(2)],
        out_specs=pl.BlockSpec((bm, bn), lambda i, j: (i, j)),
        out_shape=jax.ShapeDtypeStruct((M, D), BF16),
        compiler_params=_cparams(2),
        name=name,
    )(ya, yb, yc, wa.arr, wb.arr, wc.arr, gates, gates, gates)


def _sgu_kernel(u_ref, v_ref, lg_ref, lb_ref, ws_ref, bias_ref, o_ref, *, chunk, groups):
    vn = _layer_norm(v_ref[...], lg_ref[...], lb_ref[...]).astype(BF16)
    gd = vn.shape[1] // groups
    for c in range(vn.shape[0] // chunk):
        r0 = c * chunk
        for g in range(groups):
            c0 = g * gd
            s = jnp.dot(ws_ref[g], vn[r0:r0 + chunk, c0:c0 + gd], preferred_element_type=F32)
            s = s + bias_ref[:, c0:c0 + gd]
            o_ref[r0:r0 + chunk, c0:c0 + gd] = (u_ref[r0:r0 + chunk, c0:c0 + gd] * s).astype(o_ref.dtype)


def _sgu(uv, ln_g, ln_b, w_s, bias_full, name):
    M = uv.shape[0]
    A = uv.shape[1] // 2
    G, C, _ = w_s.shape
    bt = _blk(M, 4 * C)
    return pl.pallas_call(
        functools.partial(_sgu_kernel, chunk=C, groups=G),
        grid=(M // bt,),
        in_specs=[
            pl.BlockSpec((bt, A), lambda i: (i, 0)),
            pl.BlockSpec((bt, A), lambda i: (i, 1)),
            pl.BlockSpec((1, A), lambda i: (0, 0)),
            pl.BlockSpec((1, A), lambda i: (0, 0)),
            pl.BlockSpec((G, C, C), lambda i: (0, 0, 0)),
            pl.BlockSpec((C, A), lambda i: (0, 0)),
        ],
        out_specs=pl.BlockSpec((bt, A), lambda i: (i, 0)),
        out_shape=jax.ShapeDtypeStruct((M, A), BF16),
        compiler_params=_cparams(1),
        name=name,
    )(uv, uv, ln_g.reshape(1, A), ln_b.reshape(1, A), w_s, bias_full)


_LOG2E = 1.4426950408889634


def _attend(q, k, v, scale):
    s = lax.dot_general(q, k, _NT, preferred_element_type=F32)
    m = jnp.max(s, axis=-1, keepdims=True)
    p = jnp.exp2((s - m) * (scale * _LOG2E))
    l = jnp.sum(p, axis=-1, keepdims=True)
    return jnp.dot(p.astype(BF16), v, preferred_element_type=F32) / l


def _chunks(n_heads, rows, sub):
    out = [(h, r, sub) for h in range(n_heads) for r in range(0, rows, sub)]
    if sub >= 512 and len(out) >= 4:
        h, r, _ = out[0]
        out[0:1] = [(h, r, sub // 4), (h, r + sub // 4, sub - sub // 4)]
        h, r, _ = out[-1]
        out[-1:] = [(h, r, sub - sub // 4), (h, r + sub - sub // 4, sub // 4)]
    return out


def _gqa_kernel(q_ref, g_ref, *rest, nkv, group, hd, scale, sub, rope, ctx):
    rest = list(rest)
    cos_ref, sin_ref = (rest.pop(0), rest.pop(0)) if rope else (None, None)
    if ctx:
        ck_ref, cv_ref, k_ref, v_ref, o_ref, ks_ref, vs_ref = rest
        P = ck_ref.shape[0]
        ks_ref[:P, :] = ck_ref[...].astype(BF16)
        ks_ref[P:, :] = k_ref[...]
        vs_ref[:P, :] = cv_ref[...].astype(BF16)
        vs_ref[P:, :] = v_ref[...]
        k_ref, v_ref = ks_ref, vs_ref
    else:
        k_ref, v_ref, o_ref = rest
    for h, r, n in _chunks(nkv * group, q_ref.shape[0], sub):
        kv = h // group
        q = _rms(q_ref[r:r + n, h * hd:(h + 1) * hd], g_ref[...])
        if rope:
            q = _rope(q, cos_ref[r:r + n, :], sin_ref[r:r + n, :], hd // 4)
        o = _attend(q.astype(BF16), k_ref[:, kv * hd:(kv + 1) * hd],
                    v_ref[:, kv * hd:(kv + 1) * hd], scale)
        o_ref[r:r + n, h * hd:(h + 1) * hd] = o.astype(o_ref.dtype)


def _gqa(q, gain, rope, kv, ctx, kv_heads, hd, name, bq, nkv, sub=512):
    B, Nq, W = q.shape
    T = kv.shape[1]
    group = W // (kv_heads * hd)
    bq = _blk(Nq, bq)
    gw = nkv * group * hd
    sub = _blk(bq, sub)
    nb = kv_heads // nkv
    tab = pl.BlockSpec((bq, hd), lambda b, n, i: (i, 0))
    in_specs = [pl.BlockSpec((None, bq, gw), lambda b, n, i: (b, i, n)),
                pl.BlockSpec((1, hd), lambda b, n, i: (0, 0))]
    args = [q, gain.reshape(1, hd)]
    scratch = []
    if rope is not None:
        in_specs += [tab, tab]
        args += list(rope)
    if ctx is not None:
        P = ctx[0].shape[1]
        cspec = pl.BlockSpec((None, P, nkv * hd), lambda b, n, i: (b, 0, n))
        in_specs += [cspec, cspec]
        args += list(ctx)
        scratch = [pltpu.VMEM((P + T, nkv * hd), BF16)] * 2
    in_specs += [pl.BlockSpec((None, T, nkv * hd), lambda b, n, i: (b, 0, n)),
                 pl.BlockSpec((None, T, nkv * hd), lambda b, n, i: (b, 0, nb + n))]
    args += [kv, kv]
    return pl.pallas_call(
        functools.partial(_gqa_kernel, nkv=nkv, group=group, hd=hd, scale=hd ** -0.5, sub=sub,
                          rope=rope is not None, ctx=ctx is not None),
        grid=(B, nb, Nq // bq),
        in_specs=in_specs,
        out_specs=pl.BlockSpec((None, bq, gw), lambda b, n, i: (b, i, n)),
        out_shape=jax.ShapeDtypeStruct((B, Nq, W), BF16),
        scratch_shapes=scratch,
        compiler_params=_cparams(3),
        name=name,
    )(*args)


def _mla_kernel(q_ref, kn_ref, kpe_ref, v_ref, o_ref, *, hb, scale, sub):
    kpe = kpe_ref[...]
    ks = [jnp.concatenate([kn_ref[:, h * LANES:(h + 1) * LANES], kpe], axis=1) for h in range(hb)]
    for h, r, n in _chunks(hb, q_ref.shape[0], sub):
        o = _attend(q_ref[r:r + n, 2 * h * LANES:2 * (h + 1) * LANES], ks[h],
                    v_ref[:, h * LANES:(h + 1) * LANES], scale)
        o_ref[r:r + n, h * LANES:(h + 1) * LANES] = o.astype(o_ref.dtype)


def _mla(q, kv, kpe, heads, scale, name, bq, hb, sub=512):
    B, Nq, _ = q.shape
    S = kv.shape[1]
    bq = _blk(Nq, bq)
    sub = _blk(bq, sub)
    nhb = heads // hb
    return pl.pallas_call(
        functools.partial(_mla_kernel, hb=hb, scale=scale, sub=sub),
        grid=(B, nhb, Nq // bq),
        in_specs=[
            pl.BlockSpec((None, bq, 2 * hb * LANES), lambda b, h, i: (b, i, h)),
            pl.BlockSpec((None, S, hb * LANES), lambda b, h, i: (b, 0, h)),
            pl.BlockSpec((None, S, LANES), lambda b, h, i: (b, 0, 0)),
            pl.BlockSpec((None, S, hb * LANES), lambda b, h, i: (b, 0, nhb + h)),
        ],
        out_specs=pl.BlockSpec((None, bq, hb * LANES), lambda b, h, i: (b, i, h)),
        out_shape=jax.ShapeDtypeStruct((B, Nq, heads * LANES), BF16),
        compiler_params=_cparams(3),
        name=name,
    )(q, kv, kpe, kv)


def _rope_tables(n, dim):
    rows = n // GRID_W
    row = jnp.repeat(jnp.arange(rows, dtype=F32), GRID_W)
    col = jnp.tile(jnp.arange(GRID_W, dtype=F32), rows)
    quarter = dim // 4
    freqs = ROPE_THETA ** (-jnp.arange(quarter, dtype=F32) / quarter)
    ra = row[:, None] * freqs[None, :]
    ca = col[:, None] * freqs[None, :]
    ang = jnp.concatenate([ra, ra, ca, ca], axis=-1)
    sign = jnp.where((jnp.arange(dim) // quarter) % 2 == 0, -1.0, 1.0).astype(F32)
    cos = jnp.cos(ang)
    sin = jnp.sin(ang) * sign[None, :]
    if dim < LANES:
        cos = jnp.pad(cos, ((0, 0), (0, LANES - dim)), constant_values=1.0)
        sin = jnp.pad(sin, ((0, 0), (0, LANES - dim)))
    return cos, sin


def _stacked_weights(dims, w_in, w_s, b_s, w_uq, w_ukv, w_pa, w_pb, w_pc, w_o, w_gate, w_up, w_down):
    A, qw, kvw, qr, rank, rd, D, mh, nope, vd = dims
    L = w_in.shape[0]
    bf = lambda a: a.astype(BF16)
    o_c = 2 * A + qw + 2 * kvw + qr
    o_g = o_c + rank + rd
    s = {}
    s['in'] = bf(jnp.swapaxes(w_in, 1, 2))
    s['o_c'] = o_c
    s['g'] = s['in'][:, o_g:]
    uq = w_uq.reshape(L, qr, mh, nope + rd)
    s['uq'] = bf(jnp.pad(uq, ((0, 0), (0, 0), (0, 0), (0, 2 * LANES - nope - rd))).reshape(L, qr, mh * 2 * LANES))
    ukv = w_ukv.reshape(L, rank, mh, nope + vd)
    s['ukv'] = bf(jnp.concatenate([ukv[..., :nope].reshape(L, rank, mh * nope),
                                   ukv[..., nope:].reshape(L, rank, mh * vd)], axis=-1))
    for name, a in (('pa', w_pa), ('pb', w_pb), ('pc', w_pc), ('o', w_o), ('down', w_down), ('s', w_s)):
        s[name] = bf(a)
    s['gate'], s['up'] = w_gate, w_up
    gd = A // w_s.shape[1]
    s['sb'] = jnp.repeat(jnp.swapaxes(b_s, 1, 2), gd, axis=2)
    return s


def _layer_weights(l, dims, s):
    A, qw, kvw, qr, rank, rd, D, mh, nope, vd = dims
    o = [0, 2 * A, 2 * A + qw, 2 * A + qw + 2 * kvw, 2 * A + qw + 2 * kvw + qr]
    full = lambda a: _W(a, l, 0, a.shape[2])
    w = {}
    w['uv'] = _W(s['in'], l, o[0], 2 * A, True)
    w['q'] = _W(s['in'], l, o[1], qw, True)
    w['kv'] = _W(s['in'], l, o[2], 2 * kvw, True)
    w['cq'] = _W(s['in'], l, o[3], qr, True)
    w['ckv'] = _W(s['in'], l, s['o_c'], 2 * rank, True)
    w['g'] = _W(s['g'], l, 0, s['g'].shape[1], True)
    for name in ('uq', 'ukv', 'pa', 'pb', 'pc', 'o', 'gate', 'up', 'down'):
        w[name] = full(s[name])
    w['s'] = s['s'][l]
    w['sb'] = s['sb'][l]
    return w


def _trunk_layer(x, h, mods, w, vecs, dims, B, T, rope, ctx, alpha, nxt):
    A, qw, kvw, qr, rank, rd, D, mh, nope, vd = dims
    sgu_g, sgu_b, qg, kg, cqg, ckvg, ln1g, ln1b, ln2g, ln2b = vecs
    M = B * T
    G = mods.shape[0]
    rpg = M // G
    hd = LANES
    kvh = kvw // hd
    tag = "s" if rope is not None else "p"
    rope_g = None if rope is None else rope[0]
    rope_m = None if rope is None else rope[1]

    uv = _mm_plain(h, w['uv'], F32, "proj_uv_" + tag)
    q = _mm_plain(h, w['q'], F32, "proj_q_" + tag)
    kv_f, kv_b = _mm_heads(h, w['kv'], kg, rope_g, T, kvh, [F32, BF16], "proj_kv_" + tag,
                           bm=1024, bn=2 * kvw)
    cq = _mm_rms(h, w['cq'], cqg, BF16, "proj_cq_" + tag)
    ckv_f, ckv_b, kpe_f, kpe_b = _mm_ckv(h, w['ckv'], ckvg, rope_m, T, rank, rd, "proj_ckv_" + tag)
    gates = _mm_sigmoid(h, w['g'], BF16, "proj_gates_" + tag)

    y_a = _sgu(uv, sgu_g, sgu_b, w['s'], w['sb'], "sgu_" + tag)

    ctx_kv = None
    if ctx is not None:
        c_k, c_v, c_ckv, c_kpe = ctx
        P = c_k.shape[1]
        ctx_kv = (c_k.reshape(B, P, kvw), c_v.reshape(B, P, kvw))
    long_keys = ctx is not None
    y_b = _gqa(q.reshape(B, T, qw), qg, rope_g, kv_b.reshape(B, T, 2 * kvw), ctx_kv, kvh, hd, "gqa_" + tag,
               bq=1024 if long_keys else 256, nkv=1 if long_keys else kvh).reshape(M, qw)

    qc = _mm_uq(cq, w['uq'], rope_m, T, rd, "mla_uq_" + tag, bn=2048)
    ckv_all = ckv_b.reshape(B, T, rank)
    kpe_all = kpe_b.reshape(B, T, LANES)
    if ctx is not None:
        ckv_all = jnp.concatenate([c_ckv.astype(BF16), ckv_all], axis=1)
        c_kpe_pad = jnp.pad(c_kpe, ((0, 0), (0, 0), (0, LANES - rd))).astype(BF16)
        kpe_all = jnp.concatenate([c_kpe_pad, kpe_all], axis=1)
    S = ckv_all.shape[1]
    kvu = _mm_plain(ckv_all.reshape(B * S, rank), w['ukv'], BF16, "mla_ukv_" + tag, bn=4096)
    y_c = _mla(qc.reshape(B, T, mh * 2 * LANES), kvu.reshape(B, S, mh * (nope + vd)), kpe_all,
               mh, (nope + rd) ** -0.5, "mla_" + tag,
               bq=4096 if long_keys else 256, hb=1 if long_keys else mh).reshape(M, mh * vd)

    merged = _mm_merge(y_a, y_b, y_c, w['pa'], w['pb'], w['pc'], gates, "merge_" + tag)
    mix = _mm_plain(merged, w['o'], BF16, "proj_o_" + tag)
    x1, h2 = _ln_residual(x, mix, mods, 2, ln1g, ln1b, alpha, rpg, nxt=(mods, 4, 3))
    hid = _mm_swiglu(h2, w['gate'], w['up'], "ffn_in_" + tag)
    ff = _mm_plain(hid, w['down'], BF16, "ffn_out_" + tag, bm=512, bn=512)
    x2, h_next = _ln_residual(x1, ff, mods, 5, ln2g, ln2b, alpha, rpg, nxt=nxt)

    own = (kv_f[:, :kvw], kv_f[:, kvw:], ckv_f, kpe_f[:, :rd])
    return x2, h_next, own


def kernel(x_prompt, x_sample, cache_k, cache_v, cache_ckv, cache_kpe, c, c_ctx,
           w_ada, b_ada, w_in, sgu_ln_g, sgu_ln_b, w_s, b_s, q_norm_g, k_norm_g,
           mla_q_norm_g, mla_kv_norm_g, w_uq, w_ukv, w_pa, w_pb, w_pc, w_o,
           ln1_g, ln1_b, ln2_g, ln2_b, w_gate, w_up, w_down):
    Bp, Tp, D = x_prompt.shape
    Bs, Ts, _ = x_sample.shape
    L = w_ada.shape[0]
    A = sgu_ln_g.shape[1]
    hd = q_norm_g.shape[1]
    kvh = cache_k.shape[3]
    qw = w_pb.shape[1]
    kvw = kvh * hd
    qr = mla_q_norm_g.shape[1]
    rank = mla_kv_norm_g.shape[1]
    rd = cache_kpe.shape[-1]
    vd = LANES
    mh = w_pc.shape[1] // vd
    nope = w_uq.shape[2] // mh - rd
    assert hd == LANES and nope == LANES and w_ukv.shape[2] == mh * (nope + vd)
    dims = (A, qw, kvw, qr, rank, rd, D, mh, nope, vd)
    alpha = float((2 * L) ** 0.25)

    R = -(-(1 + Bs) // 8) * 8
    cond = jnp.concatenate([c_ctx[None, :], c, jnp.zeros((R - 1 - Bs, D), F32)], axis=0)
    mods_all = _ada(cond, w_ada, b_ada)

    rope = (_rope_tables(Ts, hd), _rope_tables(Ts, rd))

    xp = x_prompt.reshape(Bp * Tp, D)
    xs = x_sample.reshape(Bs * Ts, D)
    mods = [(mods_all[l, 0:1].reshape(1, 1, 6 * D), mods_all[l, 1:1 + Bs].reshape(Bs, 1, 6 * D))
            for l in range(L)]
    hp = _modulate(xp, mods[0][0], 1, 0, Bp * Tp)
    hs = _modulate(xs, mods[0][1], 1, 0, Ts)

    sw = _stacked_weights(dims, w_in, w_s, b_s, w_uq, w_ukv, w_pa, w_pb, w_pc, w_o, w_gate, w_up, w_down)
    new_k, new_v, new_ckv, new_kpe = [], [], [], []
    for l in range(L):
        w = _layer_weights(l, dims, sw)
        vecs = (sgu_ln_g[l], sgu_ln_b[l], q_norm_g[l], k_norm_g[l], mla_q_norm_g[l], mla_kv_norm_g[l],
                ln1_g[l], ln1_b[l], ln2_g[l], ln2_b[l])
        nxt_p = (mods[l + 1][0], 1, 0) if l + 1 < L else None
        nxt_s = (mods[l + 1][1], 1, 0) if l + 1 < L else None
        xp, hp, own = _trunk_layer(xp, hp, mods[l][0], w, vecs, dims, Bp, Tp, None, None, alpha, nxt_p)
        new_k.append(own[0].reshape(Bp, Tp, kvh, hd))
        new_v.append(own[1].reshape(Bp, Tp, kvh, hd))
        new_ckv.append(own[2].reshape(Bp, Tp, rank))
        new_kpe.append(own[3].reshape(Bp, Tp, rd))
        ctx = (cache_k[:, l], cache_v[:, l], cache_ckv[:, l], cache_kpe[:, l])
        xs, hs, _ = _trunk_layer(xs, hs, mods[l][1], w, vecs, dims, Bs, Ts, rope, ctx, alpha, nxt_s)

    return (xp.reshape(Bp, Tp, D), xs.reshape(Bs, Ts, D),
            jnp.stack(new_k, axis=1), jnp.stack(new_v, axis=1),
            jnp.stack(new_ckv, axis=1), jnp.stack(new_kpe, axis=1))
```
